```python
import math
import jax
import jax.numpy as jnp
from jax import lax
import numpy as np

D_MODEL = 1024
BATCH = 2
SEQ = 16384
DEPTH = 4
DEC_BATCH = 8
DEC_SEQ = 32
PAST_LEN = 4096

CHUNK = 64
N_META = 16
Q_BLOCK = 128
N_MIXERS = 3
NORM_EPS = 1e-6

DA_HEADS = 4
DA_DH = 64
DA_DV = 2 * DA_DH
DA_QK = 2 * DA_HEADS * DA_DH
DA_V = DA_HEADS * DA_DV
DA_IN = 2 * DA_QK + 2 * DA_V

RET_HEADS = 4
RET_DK = D_MODEL // RET_HEADS
RET_DV = 2 * RET_DK
RET_QK = RET_HEADS * RET_DK
RET_V = RET_HEADS * RET_DV
RET_IN = 2 * RET_QK + 2 * RET_V
ROPE_BASE = 10000.0

SB_HEADS = 4
SB_DH = 128
SB_W = SB_HEADS * SB_DH
SB_IN = 4 * SB_W

kernel_name = 'hybrid_stream_diff_ret_stick_step'


def rmsnorm(x, g):
    xf = x.astype(jnp.float32)
    y = xf * lax.rsqrt(jnp.mean(xf * xf, axis=-1, keepdims=True) + NORM_EPS)
    return (y * g.astype(jnp.float32)).astype(x.dtype)


def chunk_index(pos):
    return jnp.floor_divide(pos, CHUNK)


def prompt_blocks(l):
    n = l - N_META
    blocks = [(0, N_META, N_META)]
    for s in range(0, n, Q_BLOCK):
        e = N_META + min(s + Q_BLOCK, n)
        blocks.append((N_META + s, e, e))
    return blocks


def run_blocks(block_fn, q, q_pos, k_all, v_all, k_pos, blocks):
    outs = [block_fn(q[:, a:c], q_pos[a:c], k_all[:, :e], v_all[:, :e], k_pos[:e])
            for (a, c, e) in blocks]
    return jnp.concatenate(outs, axis=1)


def reverse_cumsum(x):
    k = x.shape[-1]
    pad = (-k) % Q_BLOCK
    xp = jnp.pad(x, [(0, 0)] * (x.ndim - 1) + [(pad, 0)])
    nb = (k + pad) // Q_BLOCK
    xb = xp.reshape(*x.shape[:-1], nb, Q_BLOCK)
    idx = jnp.arange(Q_BLOCK)
    upper = (idx[:, None] >= idx[None, :]).astype(x.dtype)
    within = jnp.einsum('...nj,ji->...ni', xb, upper, precision=lax.Precision.HIGHEST)
    nidx = jnp.arange(nb)
    later = (nidx[:, None] > nidx[None, :]).astype(x.dtype)
    tail = jnp.einsum('...m,mn->...n', jnp.sum(xb, axis=-1), later, precision=lax.Precision.HIGHEST)
    out = (within + tail[..., None]).reshape(*x.shape[:-1], nb * Q_BLOCK)
    return out[..., pad:]


def rotary(x, pos):
    half = x.shape[-1] // 2
    inv = ROPE_BASE ** (-jnp.linspace(0.0, 1.0, half, dtype=jnp.float32))
    ang = pos.astype(jnp.float32)[:, None] * inv[None, :]
    cos = jnp.cos(ang)[None, :, None, :]
    sin = jnp.sin(ang)[None, :, None, :]
    xf = x.astype(jnp.float32)
    x1, x2 = xf[..., :half], xf[..., half:]
    return jnp.concatenate([x1 * cos - x2 * sin, x1 * sin + x2 * cos], axis=-1).astype(x.dtype)


def diff_attention_mixer(xn, q_pos, k_pos, blocks, k_past, v_past, w_in, lam_q1, lam_k1, lam_q2, lam_k2,
                         subln_g, w_out, lam_init):
    b, l, _ = xn.shape
    proj = jnp.einsum('bld,de->ble', xn, w_in)
    q, k, v, gate = jnp.split(proj, [DA_QK, 2 * DA_QK, 2 * DA_QK + DA_V], axis=-1)
    q = q.reshape(b, l, 2 * DA_HEADS, DA_DH) * (DA_DH ** -0.5)
    k = k.reshape(b, l, 2 * DA_HEADS, DA_DH)
    v = v.reshape(b, l, DA_HEADS, DA_DV)
    k_all = k if k_past is None else jnp.concatenate([k_past, k], axis=1)
    v_all = v if v_past is None else jnp.concatenate([v_past, v], axis=1)
    lam = (jnp.exp(jnp.sum(lam_q1.astype(jnp.float32) * lam_k1.astype(jnp.float32)))
           - jnp.exp(jnp.sum(lam_q2.astype(jnp.float32) * lam_k2.astype(jnp.float32))) + lam_init)

    def block(qb, pb, kb, vb, kpb):
        nq, nk = qb.shape[1], kb.shape[1]
        s = jnp.einsum('bqnd,bknd->bnqk', qb, kb).astype(jnp.float32)
        visible = chunk_index(kpb)[None, :] <= chunk_index(pb)[:, None]
        s = jnp.where(visible, s, -jnp.inf).reshape(b, DA_HEADS, 2, nq, nk)
        e = jnp.exp(s - jnp.max(s, axis=-1, keepdims=True))
        denom = jnp.moveaxis(jnp.sum(e, axis=-1), 3, 1)
        o = jnp.einsum('bhmqk,bkhe->bqhme', e.astype(vb.dtype), vb).astype(jnp.float32)
        o = o / denom[..., None]
        return o[:, :, :, 0] - lam * o[:, :, :, 1]

    o = run_blocks(block, q, q_pos, k_all, v_all, k_pos, blocks)
    o = rmsnorm(o.astype(xn.dtype), subln_g) * (1.0 - lam_init)
    o = o.reshape(b, l, DA_V) * jax.nn.silu(gate)
    return jnp.einsum('ble,ed->bld', o, w_out), k, v


def retention_chunk(state, q, k, v, log_gamma):
    c = q.shape[1]
    idx = jnp.arange(c, dtype=jnp.float32)
    rel = idx[:, None] - idx[None, :]
    decay = jnp.where(rel >= 0, jnp.exp(jnp.maximum(rel, 0.0)[None] * log_gamma[:, None, None]), 0.0)
    scores = jnp.einsum('bihd,bjhd->bhij', q, k) * decay
    o_inner = jnp.einsum('bhij,bjhe->bihe', scores, v)
    q_dec = q * jnp.exp((idx + 1.0)[:, None] * log_gamma[None, :])[None, :, :, None]
    o_cross = jnp.einsum('bihd,bhde->bihe', q_dec, state)
    k_dec = k * jnp.exp((c - 1.0 - idx)[:, None] * log_gamma[None, :])[None, :, :, None]
    new_state = (jnp.exp(c * log_gamma)[None, :, None, None] * state
                 + jnp.einsum('bjhd,bjhe->bhde', k_dec, v))
    return new_state, o_inner + o_cross


def retention_mixer(xn, pos, state, w_in, gn_g, w_out):
    b, l, _ = xn.shape
    proj = jnp.einsum('bld,de->ble', xn, w_in)
    q, k, v, gate = jnp.split(proj, [RET_QK, 2 * RET_QK, 2 * RET_QK + RET_V], axis=-1)
    q = rotary(q.reshape(b, l, RET_HEADS, RET_DK), pos)
    k = rotary(k.reshape(b, l, RET_HEADS, RET_DK), pos) * (RET_DK ** -0.5)
    v = v.reshape(b, l, RET_HEADS, RET_DV)
    log_gamma = jnp.log(1.0 - 2.0 ** (-5.0 - jnp.arange(RET_HEADS, dtype=jnp.float32)))
    qf, kf, vf = q.astype(jnp.float32), k.astype(jnp.float32), v.astype(jnp.float32)
    if state is None:
        pad = (-l) % CHUNK
        nc = (l + pad) // CHUNK

        def to_chunks(a):
            a = jnp.pad(a, [(0, 0), (pad, 0), (0, 0), (0, 0)])
            return jnp.moveaxis(a.reshape(b, nc, CHUNK, *a.shape[2:]), 1, 0)

        s0 = jnp.zeros((b, RET_HEADS, RET_DK, RET_DV), jnp.float32)
        s_fin, o = lax.scan(lambda s, xs: retention_chunk(s, xs[0], xs[1], xs[2], log_gamma),
                            s0, (to_chunks(qf), to_chunks(kf), to_chunks(vf)))
        o = jnp.moveaxis(o, 0, 1).reshape(b, nc * CHUNK, RET_HEADS, RET_DV)[:, pad:]
    else:
        s_fin, o = retention_chunk(state.astype(jnp.float32), qf, kf, vf, log_gamma)
    o = rmsnorm(o.astype(xn.dtype), gn_g)
    o = o.reshape(b, l, RET_V) * jax.nn.silu(gate)
    return jnp.einsum('ble,ed->bld', o, w_out), s_fin.astype(xn.dtype)


def stick_breaking_mixer(xn, q_pos, k_pos, blocks, k_past, v_past, w_in, w_out):
    b, l, _ = xn.shape
    proj = jnp.einsum('bld,de->ble', xn, w_in)
    q, k, v, gate = jnp.split(proj, 4, axis=-1)
    q = q.reshape(b, l, SB_HEADS, SB_DH) * (SB_DH ** -0.5)
    k = k.reshape(b, l, SB_HEADS, SB_DH)
    v = v.reshape(b, l, SB_HEADS, SB_DH)
    k_all = k if k_past is None else jnp.concatenate([k_past, k], axis=1)
    v_all = v if v_past is None else jnp.concatenate([v_past, v], axis=1)

    def block(qb, pb, kb, vb, kpb):
        z = jnp.einsum('bqhd,bkhd->bhqk', qb, kb).astype(jnp.float32)
        earlier = kpb[None, :] < pb[:, None]
        log_keep = jnp.where(earlier, -jax.nn.softplus(z), 0.0)
        tail = reverse_cumsum(log_keep)
        a = jnp.exp(jnp.where(earlier, z + tail, -jnp.inf))
        return jnp.einsum('bhqk,bkhd->bqhd', a.astype(vb.dtype), vb)

    o = run_blocks(block, q, q_pos, k_all, v_all, k_pos, blocks)
    o = o.reshape(b, l, SB_W) * jax.nn.silu(gate)
    return jnp.einsum('ble,ed->bld', o, w_out), k, v


def setup_inputs(seed: int = 0) -> dict:
    key = jax.random.key(seed)
    keys = jax.random.split(key, 48)
    counter = [0]

    def nxt():
        kk = keys[counter[0]]
        counter[0] += 1
        return kk

    def nrm(shape, scale):
        return scale * jax.random.normal(nxt(), shape, jnp.float32)

    def gain(shape):
        return 1.0 + 0.02 * jax.random.normal(nxt(), shape, jnp.float32)

    d = D_MODEL
    return {
        'x_prompt': nrm((BATCH, SEQ, d), 1.0),
        'x_sample': nrm((DEC_BATCH, DEC_SEQ, d), 1.0),
        'cache_k_l0': nrm((DEC_BATCH, PAST_LEN, 2 * DA_HEADS, DA_DH), 1.0),
        'cache_v_l0': nrm((DEC_BATCH, PAST_LEN, DA_HEADS, DA_DV), 1.0),
        'state_ret_l1': nrm((DEC_BATCH, RET_HEADS, RET_DK, RET_DV), 0.5),
        'cache_k_l2': nrm((DEC_BATCH, PAST_LEN, SB_HEADS, SB_DH), 1.0),
        'cache_v_l2': nrm((DEC_BATCH, PAST_LEN, SB_HEADS, SB_DH), 1.0),
        'cache_k_l3': nrm((DEC_BATCH, PAST_LEN, 2 * DA_HEADS, DA_DH), 1.0),
        'cache_v_l3': nrm((DEC_BATCH, PAST_LEN, DA_HEADS, DA_DV), 1.0),
        'meta_tokens': nrm((N_META, d), 1.0),
        'norm_g_0': gain((d,)),
        'w_in_0': nrm((d, DA_IN), d ** -0.5),
        'lam_q1_0': nrm((DA_DH,), 0.1),
        'lam_k1_0': nrm((DA_DH,), 0.1),
        'lam_q2_0': nrm((DA_DH,), 0.1),
        'lam_k2_0': nrm((DA_DH,), 0.1),
        'subln_g_0': gain((DA_DV,)),
        'w_out_0': nrm((DA_V, d), DA_V ** -0.5),
        'norm_g_1': gain((d,)),
        'w_in_1': nrm((d, RET_IN), d ** -0.5),
        'gn_g_1': gain((RET_HEADS, RET_DV)),
        'w_out_1': nrm((RET_V, d), RET_V ** -0.5),
        'norm_g_2': gain((d,)),
        'w_in_2': nrm((d, SB_IN), d ** -0.5),
        'w_out_2': nrm((SB_W, d), SB_W ** -0.5),
        'norm_g_3': gain((d,)),
        'w_in_3': nrm((d, DA_IN), d ** -0.5),
        'lam_q1_3': nrm((DA_DH,), 0.1),
        'lam_k1_3': nrm((DA_DH,), 0.1),
        'lam_q2_3': nrm((DA_DH,), 0.1),
        'lam_k2_3': nrm((DA_DH,), 0.1),
        'subln_g_3': gain((DA_DV,)),
        'w_out_3': nrm((DA_V, d), DA_V ** -0.5),
        'norm_g_final': gain((d,)),
    }


def reference(x_prompt, x_sample, cache_k_l0, cache_v_l0, state_ret_l1, cache_k_l2, cache_v_l2,
              cache_k_l3, cache_v_l3, meta_tokens,
              norm_g_0, w_in_0, lam_q1_0, lam_k1_0, lam_q2_0, lam_k2_0, subln_g_0, w_out_0,
              norm_g_1, w_in_1, gn_g_1, w_out_1,
              norm_g_2, w_in_2, w_out_2,
              norm_g_3, w_in_3, lam_q1_3, lam_k1_3, lam_q2_3, lam_k2_3, subln_g_3, w_out_3,
              norm_g_final):
    b = x_prompt.shape[0]
    meta = jnp.broadcast_to(meta_tokens.astype(x_prompt.dtype)[None], (b, N_META, D_MODEL))
    xp = jnp.concatenate([meta, x_prompt], axis=1)
    xs = x_sample
    l_p = xp.shape[1]
    s_len = xs.shape[1]
    past = cache_k_l0.shape[1]
    pos_p = jnp.arange(l_p, dtype=jnp.int32) - N_META
    pos_s = past + jnp.arange(s_len, dtype=jnp.int32)
    kpos_s = jnp.arange(past + s_len, dtype=jnp.int32)
    blocks_p = prompt_blocks(l_p)
    blocks_s = [(0, s_len, past + s_len)]

    layers = [
        (norm_g_0, w_in_0, w_out_0, (lam_q1_0, lam_k1_0, lam_q2_0, lam_k2_0, subln_g_0), (cache_k_l0, cache_v_l0)),
        (norm_g_1, w_in_1, w_out_1, (gn_g_1,), (state_ret_l1,)),
        (norm_g_2, w_in_2, w_out_2, (), (cache_k_l2, cache_v_l2)),
        (norm_g_3, w_in_3, w_out_3, (lam_q1_3, lam_k1_3, lam_q2_3, lam_k2_3, subln_g_3), (cache_k_l3, cache_v_l3)),
    ]
    new_states = []
    for i in range(DEPTH):
        norm_g, w_in, w_out, extra, cache = layers[i]
        kind = i % N_MIXERS
        hp = rmsnorm(xp, norm_g)
        hs = rmsnorm(xs, norm_g)
        if kind == 0:
            lam_init = 0.8 - 0.6 * math.exp(-0.3 * i)
            op, kp, vp = diff_attention_mixer(hp, pos_p, pos_p, blocks_p, None, None, w_in, *extra, w_out, lam_init)
            os_, ks_, vs_ = diff_attention_mixer(hs, pos_s, kpos_s, blocks_s, cache[0], cache[1], w_in, *extra,
                                                 w_out, lam_init)
            new_states.append((kp, vp, ks_, vs_))
        elif kind == 1:
            op, sp = retention_mixer(hp, pos_p, None, w_in, extra[0], w_out)
            os_, ss = retention_mixer(hs, pos_s, cache[0], w_in, extra[0], w_out)
            new_states.append((sp, ss))
        else:
            op, kp, vp = stick_breaking_mixer(hp, pos_p, pos_p, blocks_p, None, None, w_in, w_out)
            os_, ks_, vs_ = stick_breaking_mixer(hs, pos_s, kpos_s, blocks_s, cache[0], cache[1], w_in, w_out)
            new_states.append((kp, vp, ks_, vs_))
        xp = xp + op
        xs = xs + os_

    y_prompt = rmsnorm(xp, norm_g_final)[:, N_META:]
    y_sample = rmsnorm(xs, norm_g_final)
    k0_p, v0_p, k0_s, v0_s = new_states[0]
    ret1_p, ret1_s = new_states[1]
    k2_p, v2_p, k2_s, v2_s = new_states[2]
    k3_p, v3_p, k3_s, v3_s = new_states[3]
    return (y_prompt, y_sample, k0_p, v0_p, k0_s, v0_s, ret1_p, ret1_s,
            k2_p, v2_p, k2_s, v2_s, k3_p, v3_p, k3_s, v3_s)
```

```python
import functools
import math

import jax
import jax.numpy as jnp
from jax import lax
from jax.experimental import pallas as pl
from jax.experimental.pallas import tpu as pltpu

F32 = jnp.float32
BF16 = jnp.bfloat16

CHUNK = 64
N_META = 16
NORM_EPS = 1e-6
N_MIXERS = 3
ROPE_BASE = 10000.0

DA_HEADS = 4
DA_DH = 64
DA_DV = 128
DA_W = 512
RET_HEADS = 4
RET_DK = 256
RET_DV = 512
SB_HEADS = 4
SB_DH = 128
SB_W = 512

LANES = 128
FRONT_PAD = 512
ATT_TILE = 256
RET_BLOCK = 256
ROW_TILE = 512
VMEM_LIMIT = 48 * 1024 * 1024

NEG_INF = float("-inf")
NT_DIMS = (((1,), (1,)), ((), ()))
TN_DIMS = (((0,), (0,)), ((), ()))


def _cparams(sem):
    return pltpu.CompilerParams(dimension_semantics=sem, vmem_limit_bytes=VMEM_LIMIT)


def _silu(g):
    return g / (1.0 + jnp.exp(-g))


def _softplus(z):
    return jnp.maximum(z, 0.0) + jnp.log(1.0 + jnp.exp(-jnp.abs(z)))


def _norm_proj_kernel(x_ref, g_ref, w_ref, o_ref):
    x = x_ref[...]
    ms = jnp.mean(x * x, axis=-1, keepdims=True)
    h = (x * lax.rsqrt(ms + NORM_EPS) * g_ref[...]).astype(BF16)
    o_ref[...] = jnp.dot(h, w_ref[...], preferred_element_type=F32)


def norm_proj(x, g, w_bf16, *, col_tile=2048):
    rows, d = x.shape
    e = w_bf16.shape[1]
    tr = min(ROW_TILE, rows)
    te = min(col_tile, e)
    return pl.pallas_call(
        _norm_proj_kernel,
        out_shape=jax.ShapeDtypeStruct((rows, e), F32),
        grid=(e // te, rows // tr),
        in_specs=[
            pl.BlockSpec((tr, d), lambda j, i: (i, 0)),
            pl.BlockSpec((1, d), lambda j, i: (0, 0)),
            pl.BlockSpec((d, te), lambda j, i: (0, j)),
        ],
        out_specs=pl.BlockSpec((tr, te), lambda j, i: (i, j)),
        compiler_params=_cparams(("parallel", "parallel")),
        name="norm_proj",
    )(x, g.reshape(1, d), w_bf16)


def _post_kernel(o_ref, gate_ref, x_ref, w_ref, out_ref):
    a = (o_ref[...] * _silu(gate_ref[...])).astype(BF16)
    out_ref[...] = x_ref[...] + jnp.dot(a, w_ref[...], preferred_element_type=F32)


def _post_final_kernel(o_ref, gate_ref, x_ref, w_ref, gf_ref, out_ref):
    a = (o_ref[...] * _silu(gate_ref[...])).astype(BF16)
    xn = x_ref[...] + jnp.dot(a, w_ref[...], preferred_element_type=F32)
    ms = jnp.mean(xn * xn, axis=-1, keepdims=True)
    out_ref[...] = xn * lax.rsqrt(ms + NORM_EPS) * gf_ref[...]


def post_proj(o, proj, gate_block, x, w_bf16, *, batch=1, final_g=None, skip_rows=0):
    rows, wdt = o.shape
    d = x.shape[1]
    per = rows // batch
    tr = min(ROW_TILE, per)
    off = skip_rows // tr
    n_in = per // tr
    n = n_in - off
    in_specs = [
        pl.BlockSpec((tr, wdt), lambda bb, i: (bb * n_in + off + i, 0)),
        pl.BlockSpec((tr, wdt), lambda bb, i: (bb * n_in + off + i, gate_block)),
        pl.BlockSpec((tr, d), lambda bb, i: (bb * n_in + off + i, 0)),
        pl.BlockSpec((wdt, d), lambda bb, i: (0, 0)),
    ]
    args = [o, proj, x, w_bf16]
    kern = _post_kernel
    if final_g is not None:
        in_specs.append(pl.BlockSpec((1, d), lambda bb, i: (0, 0)))
        args.append(final_g.reshape(1, d))
        kern = _post_final_kernel
    return pl.pallas_call(
        kern,
        out_shape=jax.ShapeDtypeStruct((batch * n * tr, d), F32),
        grid=(batch, n),
        in_specs=in_specs,
        out_specs=pl.BlockSpec((tr, d), lambda bb, i: (bb * n + i, 0)),
        compiler_params=_cparams(("parallel", "parallel")),
        name="post_proj",
    )(*args)


def _rope_kernel(inv_ref, cos_ref, sin_ref, *, start):
    tr = cos_ref.shape[0]
    row = pl.program_id(0) * tr + lax.broadcasted_iota(jnp.int32, (tr, LANES), 0)
    ang = (row + start).astype(F32) * inv_ref[...]
    cos_ref[...] = jnp.cos(ang)
    sin_ref[...] = jnp.sin(ang)


def rope_tables(n, start):
    half = RET_DK // 2
    inv = (ROPE_BASE ** (-jnp.linspace(0.0, 1.0, half, dtype=F32))).reshape(1, half)
    tr = min(ROW_TILE, n)
    return pl.pallas_call(
        functools.partial(_rope_kernel, start=start),
        out_shape=(jax.ShapeDtypeStruct((n, half), F32), jax.ShapeDtypeStruct((n, half), F32)),
        grid=(n // tr,),
        in_specs=[pl.BlockSpec((1, half), lambda i: (0, 0))],
        out_specs=(pl.BlockSpec((tr, half), lambda i: (i, 0)), pl.BlockSpec((tr, half), lambda i: (i, 0))),
        compiler_params=_cparams(("parallel",)),
        name="rope_tables",
    )(inv)


def _rotary(x, cos, sin):
    half = RET_DK // 2
    x1, x2 = x[:, :half], x[:, half:]
    return jnp.concatenate([x1 * cos - x2 * sin, x1 * sin + x2 * cos], axis=1)


def _da_split_q(q):
    rows = q.shape[0]
    lane = lax.broadcasted_iota(jnp.int32, (rows, LANES), 1)
    out = []
    for h in range(DA_HEADS):
        qh = q[:, h * LANES:(h + 1) * LANES]
        qa = jnp.where(lane < DA_DH, qh, 0.0)
        qb = jnp.where(lane >= DA_DH, qh, 0.0)
        out.append(jnp.concatenate([qa, qb], axis=0).astype(BF16))
    return out


def _da_update(h, qh, kh, vh, vis, m_s, l_s, acc_s):
    s = lax.dot_general(qh, kh, NT_DIMS, preferred_element_type=F32)
    if vis is not None:
        s = jnp.where(vis, s, NEG_INF)
    m_prev = m_s[h]
    m_new = jnp.maximum(m_prev, jnp.max(s, axis=1, keepdims=True))
    m_use = jnp.where(m_new == NEG_INF, 0.0, m_new)
    alpha = jnp.exp(m_prev - m_use)
    e = jnp.exp(s - m_use)
    l_s[h] = alpha * l_s[h] + jnp.sum(e, axis=1, keepdims=True)
    acc_s[h] = alpha * acc_s[h] + jnp.dot(e.astype(BF16), vh, preferred_element_type=F32)
    m_s[h] = m_new


def _da_finish(lam_ref, g_ref, o_ref, l_s, acc_s, rows, lam_init):
    lam = (jnp.exp(jnp.sum(lam_ref[0:1, :] * lam_ref[1:2, :], axis=1, keepdims=True))
           - jnp.exp(jnp.sum(lam_ref[2:3, :] * lam_ref[3:4, :], axis=1, keepdims=True)) + lam_init)
    for h in range(DA_HEADS):
        l = l_s[h]
        o = acc_s[h] / jnp.where(l == 0.0, 1.0, l)
        d = o[:rows] - lam * o[rows:]
        ms = jnp.mean(d * d, axis=1, keepdims=True)
        o_ref[:, h * DA_DV:(h + 1) * DA_DV] = d * lax.rsqrt(ms + NORM_EPS) * g_ref[...] * (1.0 - lam_init)


def _da_prompt_kernel(qi_ref, ki_ref, q_ref, k_ref, v_ref, lam_ref, g_ref, o_ref,
                      qs, m_s, l_s, acc_s, *, tile, lam_init):
    step = pl.program_id(1)
    qi = qi_ref[step]
    ki = ki_ref[step]

    @pl.when(ki == 0)
    def _():
        for h, qh in enumerate(_da_split_q(q_ref[...] * (DA_DH ** -0.5))):
            qs[h] = qh
        m_s[...] = jnp.full(m_s.shape, NEG_INF, F32)
        l_s[...] = jnp.zeros(l_s.shape, F32)
        acc_s[...] = jnp.zeros(acc_s.shape, F32)

    def run(masked):
        vis = None
        if masked:
            pq = qi * tile + (lax.broadcasted_iota(jnp.int32, (2 * tile, tile), 0) & (tile - 1))
            pk = ki * tile + lax.broadcasted_iota(jnp.int32, (2 * tile, tile), 1)
            vis = (pk >= FRONT_PAD - N_META) & ((pk >> 6) <= (pq >> 6))
        for h in range(DA_HEADS):
            kh = k_ref[:, h * LANES:(h + 1) * LANES].astype(BF16)
            vh = v_ref[:, h * DA_DV:(h + 1) * DA_DV].astype(BF16)
            _da_update(h, qs[h], kh, vh, vis, m_s, l_s, acc_s)

    needs_mask = (ki == qi) | (ki == 0)
    pl.when(needs_mask)(lambda: run(True))
    pl.when(jnp.logical_not(needs_mask))(lambda: run(False))

    @pl.when(ki == qi)
    def _():
        _da_finish(lam_ref, g_ref, o_ref, l_s, acc_s, tile, lam_init)


def _tri_steps(n, descending):
    qi, ki = [], []
    for i in range(n):
        ks = range(i, -1, -1) if descending else range(i + 1)
        for k in ks:
            qi.append(i)
            ki.append(k)
    return jnp.asarray(qi, jnp.int32), jnp.asarray(ki, jnp.int32)


def da_prompt(proj, lam4, subln_g, lam_init):
    b, lp, _ = proj.shape
    t = ATT_TILE
    qi, ki = _tri_steps(lp // t, descending=False)
    kern = functools.partial(_da_prompt_kernel, tile=t, lam_init=lam_init)
    grid_spec = pltpu.PrefetchScalarGridSpec(
        num_scalar_prefetch=2,
        grid=(b, qi.shape[0]),
        in_specs=[
            pl.BlockSpec((None, t, DA_W), lambda bb, s, qi, ki: (bb, qi[s], 0)),
            pl.BlockSpec((None, t, DA_W), lambda bb, s, qi, ki: (bb, ki[s], 1)),
            pl.BlockSpec((None, t, DA_W), lambda bb, s, qi, ki: (bb, ki[s], 2)),
            pl.BlockSpec((4, DA_DH), lambda bb, s, qi, ki: (0, 0)),
            pl.BlockSpec((1, DA_DV), lambda bb, s, qi, ki: (0, 0)),
        ],
        out_specs=pl.BlockSpec((None, t, DA_W), lambda bb, s, qi, ki: (bb, qi[s], 0)),
        scratch_shapes=[
            pltpu.VMEM((DA_HEADS, 2 * t, LANES), BF16),
            pltpu.VMEM((DA_HEADS, 2 * t, 1), F32),
            pltpu.VMEM((DA_HEADS, 2 * t, 1), F32),
            pltpu.VMEM((DA_HEADS, 2 * t, DA_DV), F32),
        ],
    )
    return pl.pallas_call(
        kern,
        out_shape=jax.ShapeDtypeStruct((b, lp, DA_W), F32),
        grid_spec=grid_spec,
        compiler_params=_cparams(("parallel", "arbitrary")),
        name="da_prompt",
    )(qi, ki, proj, proj, proj, lam4, subln_g.reshape(1, DA_DV))


def _da_sample_kernel(q_ref, kn_ref, vn_ref, kc_ref, vc_ref, lam_ref, g_ref, o_ref,
                      qs, m_s, l_s, acc_s, *, rows, lam_init):
    j = pl.program_id(1)

    @pl.when(j == 0)
    def _():
        for h, qh in enumerate(_da_split_q(q_ref[...] * (DA_DH ** -0.5))):
            qs[h] = qh
        m_s[...] = jnp.full(m_s.shape, NEG_INF, F32)
        l_s[...] = jnp.zeros(l_s.shape, F32)
        acc_s[...] = jnp.zeros(acc_s.shape, F32)

    for h in range(DA_HEADS):
        kh = kc_ref[:, h * LANES:(h + 1) * LANES].astype(BF16)
        vh = vc_ref[:, h * DA_DV:(h + 1) * DA_DV].astype(BF16)
        _da_update(h, qs[h], kh, vh, None, m_s, l_s, acc_s)

    @pl.when(j == pl.num_programs(1) - 1)
    def _():
        for h in range(DA_HEADS):
            kh = kn_ref[:, h * LANES:(h + 1) * LANES].astype(BF16)
            vh = vn_ref[:, h * DA_DV:(h + 1) * DA_DV].astype(BF16)
            _da_update(h, qs[h], kh, vh, None, m_s, l_s, acc_s)
        _da_finish(lam_ref, g_ref, o_ref, l_s, acc_s, rows, lam_init)


def da_sample(proj, cache_k, cache_v, lam4, subln_g, lam_init):
    db, s, _ = proj.shape
    past = cache_k.shape[1]
    tk = min(1024, past)
    kern = functools.partial(_da_sample_kernel, rows=s, lam_init=lam_init)
    return pl.pallas_call(
        kern,
        out_shape=jax.ShapeDtypeStruct((db, s, DA_W), F32),
        grid=(db, past // tk),
        in_specs=[
            pl.BlockSpec((None, s, DA_W), lambda b, j: (b, 0, 0)),
            pl.BlockSpec((None, s, DA_W), lambda b, j: (b, 0, 1)),
            pl.BlockSpec((None, s, DA_W), lambda b, j: (b, 0, 2)),
            pl.BlockSpec((None, tk, DA_W), lambda b, j: (b, j, 0)),
            pl.BlockSpec((None, tk, DA_W), lambda b, j: (b, j, 0)),
            pl.BlockSpec((4, DA_DH), lambda b, j: (0, 0)),
            pl.BlockSpec((1, DA_DV), lambda b, j: (0, 0)),
        ],
        out_specs=pl.BlockSpec((None, s, DA_W), lambda b, j: (b, 0, 0)),
        scratch_shapes=[
            pltpu.VMEM((DA_HEADS, 2 * s, LANES), BF16),
            pltpu.VMEM((DA_HEADS, 2 * s, 1), F32),
            pltpu.VMEM((DA_HEADS, 2 * s, 1), F32),
            pltpu.VMEM((DA_HEADS, 2 * s, DA_DV), F32),
        ],
        compiler_params=_cparams(("parallel", "arbitrary")),
        name="da_sample",
    )(proj, proj, proj, cache_k, cache_v, lam4, subln_g.reshape(1, DA_DV))


def _sb_update(h, qh, kh, vh, u, vis, r_s, acc_s):
    z = lax.dot_general(qh, kh, NT_DIMS, preferred_element_type=F32)
    lk = -_softplus(z)
    if vis is not None:
        lk = jnp.where(vis, lk, 0.0)
    hi = lk.astype(BF16)
    lo = (lk - hi.astype(F32)).astype(BF16)
    tail = (jnp.dot(hi, u, preferred_element_type=F32) + jnp.dot(lo, u, preferred_element_type=F32)) + r_s[h]
    a = jnp.exp(z + tail)
    if vis is not None:
        a = jnp.where(vis, a, 0.0)
    acc_s[h] = acc_s[h] + jnp.dot(a.astype(BF16), vh, preferred_element_type=F32)
    r_s[h] = r_s[h] + jnp.sum(lk, axis=1, keepdims=True)


def _sb_prompt_kernel(qi_ref, ki_ref, q_ref, k_ref, v_ref, u_ref, o_ref, qs, r_s, acc_s, *, tile):
    step = pl.program_id(1)
    qi = qi_ref[step]
    ki = ki_ref[step]

    @pl.when(ki == qi)
    def _():
        q = q_ref[...] * (SB_DH ** -0.5)
        for h in range(SB_HEADS):
            qs[h] = q[:, h * SB_DH:(h + 1) * SB_DH].astype(BF16)
        r_s[...] = jnp.zeros(r_s.shape, F32)
        acc_s[...] = jnp.zeros(acc_s.shape, F32)

    def run(masked):
        vis = None
        if masked:
            pq = qi * tile + lax.broadcasted_iota(jnp.int32, (tile, tile), 0)
            pk = ki * tile + lax.broadcasted_iota(jnp.int32, (tile, tile), 1)
            vis = (pk >= FRONT_PAD - N_META) & (pk < pq)
        u = u_ref[...]
        for h in range(SB_HEADS):
            kh = k_ref[:, h * SB_DH:(h + 1) * SB_DH].astype(BF16)
            vh = v_ref[:, h * SB_DH:(h + 1) * SB_DH].astype(BF16)
            _sb_update(h, qs[h], kh, vh, u, vis, r_s, acc_s)

    needs_mask = (ki == qi) | (ki == 0)
    pl.when(needs_mask)(lambda: run(True))
    pl.when(jnp.logical_not(needs_mask))(lambda: run(False))

    @pl.when(ki == 0)
    def _():
        for h in range(SB_HEADS):
            o_ref[:, h * SB_DH:(h + 1) * SB_DH] = acc_s[h]


def _rev_tri(n):
    idx = jnp.arange(n)
    return (idx[:, None] >= idx[None, :]).astype(BF16)


def sb_prompt(proj):
    b, lp, _ = proj.shape
    t = ATT_TILE
    qi, ki = _tri_steps(lp // t, descending=True)
    kern = functools.partial(_sb_prompt_kernel, tile=t)
    grid_spec = pltpu.PrefetchScalarGridSpec(
        num_scalar_prefetch=2,
        grid=(b, qi.shape[0]),
        in_specs=[
            pl.BlockSpec((None, t, SB_W), lambda bb, s, qi, ki: (bb, qi[s], 0)),
            pl.BlockSpec((None, t, SB_W), lambda bb, s, qi, ki: (bb, ki[s], 1)),
            pl.BlockSpec((None, t, SB_W), lambda bb, s, qi, ki: (bb, ki[s], 2)),
            pl.BlockSpec((t, t), lambda bb, s, qi, ki: (0, 0)),
        ],
        out_specs=pl.BlockSpec((None, t, SB_W), lambda bb, s, qi, ki: (bb, qi[s], 0)),
        scratch_shapes=[
            pltpu.VMEM((SB_HEADS, t, SB_DH), BF16),
            pltpu.VMEM((SB_HEADS, t, 1), F32),
            pltpu.VMEM((SB_HEADS, t, SB_DH), F32),
        ],
    )
    return pl.pallas_call(
        kern,
        out_shape=jax.ShapeDtypeStruct((b, lp, SB_W), F32),
        grid_spec=grid_spec,
        compiler_params=_cparams(("parallel", "arbitrary")),
        name="sb_prompt",
    )(qi, ki, proj, proj, proj, _rev_tri(t))


def _sb_sample_kernel(q_ref, kn_ref, vn_ref, kc_ref, vc_ref, un_ref, uc_ref, o_ref, qs, r_s, acc_s, *, rows):
    j = pl.program_id(1)

    @pl.when(j == 0)
    def _():
        q = q_ref[...] * (SB_DH ** -0.5)
        for h in range(SB_HEADS):
            qs[h] = q[:, h * SB_DH:(h + 1) * SB_DH].astype(BF16)
        r_s[...] = jnp.zeros(r_s.shape, F32)
        acc_s[...] = jnp.zeros(acc_s.shape, F32)
        iq = lax.broadcasted_iota(jnp.int32, (rows, rows), 0)
        ik = lax.broadcasted_iota(jnp.int32, (rows, rows), 1)
        vis = ik < iq
        un = un_ref[...]
        for h in range(SB_HEADS):
            kh = kn_ref[:, h * SB_DH:(h + 1) * SB_DH].astype(BF16)
            vh = vn_ref[:, h * SB_DH:(h + 1) * SB_DH].astype(BF16)
            _sb_update(h, qs[h], kh, vh, un, vis, r_s, acc_s)

    uc = uc_ref[...]
    for h in range(SB_HEADS):
        kh = kc_ref[:, h * SB_DH:(h + 1) * SB_DH].astype(BF16)
        vh = vc_ref[:, h * SB_DH:(h + 1) * SB_DH].astype(BF16)
        _sb_update(h, qs[h], kh, vh, uc, None, r_s, acc_s)

    @pl.when(j == pl.num_programs(1) - 1)
    def _():
        for h in range(SB_HEADS):
            o_ref[:, h * SB_DH:(h + 1) * SB_DH] = acc_s[h]


def sb_sample(proj, cache_k, cache_v):
    db, s, _ = proj.shape
    past = cache_k.shape[1]
    tk = min(256, past)
    n = past // tk
    kern = functools.partial(_sb_sample_kernel, rows=s)
    return pl.pallas_call(
        kern,
        out_shape=jax.ShapeDtypeStruct((db, s, SB_W), F32),
        grid=(db, n),
        in_specs=[
            pl.BlockSpec((None, s, SB_W), lambda b, j: (b, 0, 0)),
            pl.BlockSpec((None, s, SB_W), lambda b, j: (b, 0, 1)),
            pl.BlockSpec((None, s, SB_W), lambda b, j: (b, 0, 2)),
            pl.BlockSpec((None, tk, SB_W), lambda b, j: (b, n - 1 - j, 0)),
            pl.BlockSpec((None, tk, SB_W), lambda b, j: (b, n - 1 - j, 0)),
            pl.BlockSpec((s, s), lambda b, j: (0, 0)),
            pl.BlockSpec((tk, tk), lambda b, j: (0, 0)),
        ],
        out_specs=pl.BlockSpec((None, s, SB_W), lambda b, j: (b, 0, 0)),
        scratch_shapes=[
            pltpu.VMEM((SB_HEADS, s, SB_DH), BF16),
            pltpu.VMEM((SB_HEADS, s, 1), F32),
            pltpu.VMEM((SB_HEADS, s, SB_DH), F32),
        ],
        compiler_params=_cparams(("parallel", "arbitrary")),
        name="sb_sample",
    )(proj, proj, proj, cache_k, cache_v, _rev_tri(s), _rev_tri(tk))


def _ret_block(q, k, v, cos, sin, state, log_gamma):
    c = q.shape[0]
    qr = _rotary(q, cos, sin)
    kr = _rotary(k, cos, sin) * (RET_DK ** -0.5)
    ii = lax.broadcasted_iota(jnp.int32, (c, c), 0)
    jj = lax.broadcasted_iota(jnp.int32, (c, c), 1)
    rel = (ii - jj).astype(F32)
    decay = jnp.where(rel >= 0.0, jnp.exp(jnp.maximum(rel, 0.0) * log_gamma), 0.0)
    idx = lax.broadcasted_iota(jnp.int32, (c, 1), 0).astype(F32)
    q_dec = (qr * jnp.exp((idx + 1.0) * log_gamma)).astype(BF16)
    k_dec = (kr * jnp.exp((c - 1.0 - idx) * log_gamma)).astype(BF16)
    vb = v.astype(BF16)
    scores = lax.dot_general(qr.astype(BF16), kr.astype(BF16), NT_DIMS, preferred_element_type=F32) * decay
    o = (jnp.dot(scores.astype(BF16), vb, preferred_element_type=F32)
         + jnp.dot(q_dec, state.astype(BF16), preferred_element_type=F32))
    new_state = (jnp.exp(c * log_gamma) * state
                 + lax.dot_general(k_dec, vb, TN_DIMS, preferred_element_type=F32))
    return o, new_state


def _ret_norm(o, g):
    ms = jnp.mean(o * o, axis=1, keepdims=True)
    return o * lax.rsqrt(ms + NORM_EPS) * g


def _log_gamma(h):
    hv = jnp.zeros((1, 1), F32) + h.astype(F32)
    return jnp.log(1.0 - jnp.exp2(-5.0 - hv))


def _ret_prompt_kernel(q_ref, k_ref, v_ref, cos_ref, sin_ref, g_ref, o_ref, sf_ref, st):
    j = pl.program_id(2)

    @pl.when(j == 0)
    def _():
        st[...] = jnp.zeros(st.shape, F32)

    o, new_state = _ret_block(q_ref[...], k_ref[...], v_ref[...], cos_ref[...], sin_ref[...], st[...],
                              _log_gamma(pl.program_id(1)))
    o_ref[...] = _ret_norm(o, g_ref[...])
    st[...] = new_state

    @pl.when(j == pl.num_programs(2) - 1)
    def _():
        sf_ref[...] = new_state


def ret_prompt(proj, cos, sin, gn_g):
    b, lp, _ = proj.shape
    c = RET_BLOCK
    kv_off = (RET_HEADS * RET_DK) // RET_DK
    v_off = (2 * RET_HEADS * RET_DK) // RET_DV
    return pl.pallas_call(
        _ret_prompt_kernel,
        out_shape=(jax.ShapeDtypeStruct((b, lp, RET_HEADS * RET_DV), F32),
                   jax.ShapeDtypeStruct((b, RET_HEADS, RET_DK, RET_DV), F32)),
        grid=(b, RET_HEADS, lp // c),
        in_specs=[
            pl.BlockSpec((None, c, RET_DK), lambda bb, h, j: (bb, j, h)),
            pl.BlockSpec((None, c, RET_DK), lambda bb, h, j: (bb, j, kv_off + h)),
            pl.BlockSpec((None, c, RET_DV), lambda bb, h, j: (bb, j, v_off + h)),
            pl.BlockSpec((c, RET_DK // 2), lambda bb, h, j: (j, 0)),
            pl.BlockSpec((c, RET_DK // 2), lambda bb, h, j: (j, 0)),
            pl.BlockSpec((None, 1, RET_DV), lambda bb, h, j: (h, 0, 0)),
        ],
        out_specs=(pl.BlockSpec((None, c, RET_DV), lambda bb, h, j: (bb, j, h)),
                   pl.BlockSpec((None, None, RET_DK, RET_DV), lambda bb, h, j: (bb, h, 0, 0))),
        scratch_shapes=[pltpu.VMEM((RET_DK, RET_DV), F32)],
        compiler_params=_cparams(("parallel", "parallel", "arbitrary")),
        name="ret_prompt",
    )(proj, proj, proj, cos, sin, gn_g.reshape(RET_HEADS, 1, RET_DV))


def _ret_sample_kernel(q_ref, k_ref, v_ref, cos_ref, sin_ref, g_ref, s_ref, o_ref, sf_ref):
    o, new_state = _ret_block(q_ref[...], k_ref[...], v_ref[...], cos_ref[...], sin_ref[...], s_ref[...],
                              _log_gamma(pl.program_id(1)))
    o_ref[...] = _ret_norm(o, g_ref[...])
    sf_ref[...] = new_state


def ret_sample(proj, cos, sin, gn_g, state):
    db, s, _ = proj.shape
    kv_off = (RET_HEADS * RET_DK) // RET_DK
    v_off = (2 * RET_HEADS * RET_DK) // RET_DV
    return pl.pallas_call(
        _ret_sample_kernel,
        out_shape=(jax.ShapeDtypeStruct((db, s, RET_HEADS * RET_DV), F32),
                   jax.ShapeDtypeStruct(state.shape, F32)),
        grid=(db, RET_HEADS),
        in_specs=[
            pl.BlockSpec((None, s, RET_DK), lambda b, h: (b, 0, h)),
            pl.BlockSpec((None, s, RET_DK), lambda b, h: (b, 0, kv_off + h)),
            pl.BlockSpec((None, s, RET_DV), lambda b, h: (b, 0, v_off + h)),
            pl.BlockSpec((s, RET_DK // 2), lambda b, h: (0, 0)),
            pl.BlockSpec((s, RET_DK // 2), lambda b, h: (0, 0)),
            pl.BlockSpec((None, 1, RET_DV), lambda b, h: (h, 0, 0)),
            pl.BlockSpec((None, None, RET_DK, RET_DV), lambda b, h: (b, h, 0, 0)),
        ],
        out_specs=(pl.BlockSpec((None, s, RET_DV), lambda b, h: (b, 0, h)),
                   pl.BlockSpec((None, None, RET_DK, RET_DV), lambda b, h: (b, h, 0, 0))),
        compiler_params=_cparams(("parallel", "parallel")),
        name="ret_sample",
    )(proj, proj, proj, cos, sin, gn_g.reshape(RET_HEADS, 1, RET_DV), state)


def kernel(x_prompt, x_sample, cache_k_l0, cache_v_l0, state_ret_l1, cache_k_l2, cache_v_l2, cache_k_l3, cache_v_l3, meta_tokens, norm_g_0, w_in_0, lam_q1_0, lam_k1_0, lam_q2_0, lam_k2_0, subln_g_0, w_out_0, norm_g_1, w_in_1, gn_g_1, w_out_1, norm_g_2, w_in_2, w_out_2, norm_g_3, w_in_3, lam_q1_3, lam_k1_3, lam_q2_3, lam_k2_3, subln_g_3, w_out_3, norm_g_final):
    b, seq, d = x_prompt.shape
    db, s_len, _ = x_sample.shape
    past = cache_k_l0.shape[1]
    assert seq % FRONT_PAD == 0 and N_META <= FRONT_PAD
    lp = FRONT_PAD + seq
    n_real = N_META + seq
    first = FRONT_PAD - N_META

    meta = jnp.broadcast_to(meta_tokens.astype(F32)[None], (b, N_META, d))
    xp = jnp.concatenate([jnp.zeros((b, first, d), F32), meta, x_prompt], axis=1).reshape(b * lp, d)
    xs = x_sample.reshape(db * s_len, d)

    layers = [
        (norm_g_0, w_in_0, w_out_0, (lam_q1_0, lam_k1_0, lam_q2_0, lam_k2_0, subln_g_0), (cache_k_l0, cache_v_l0)),
        (norm_g_1, w_in_1, w_out_1, (gn_g_1,), (state_ret_l1,)),
        (norm_g_2, w_in_2, w_out_2, (), (cache_k_l2, cache_v_l2)),
        (norm_g_3, w_in_3, w_out_3, (lam_q1_3, lam_k1_3, lam_q2_3, lam_k2_3, subln_g_3), (cache_k_l3, cache_v_l3)),
    ]
    states = []
    y_prompt = y_sample = None
    n_layers = len(layers)
    for i, (norm_g, w_in, w_out, extra, cache) in enumerate(layers):
        kind = i % N_MIXERS
        w_in_b = w_in.astype(BF16)
        w_out_b = w_out.astype(BF16)
        pp = norm_proj(xp, norm_g, w_in_b)
        ps = norm_proj(xs, norm_g, w_in_b)
        e = w_in.shape[1]
        pp3 = pp.reshape(b, lp, e)
        ps3 = ps.reshape(db, s_len, e)
        if kind == 0:
            lam_init = 0.8 - 0.6 * math.exp(-0.3 * i)
            lam4 = jnp.stack([v.astype(F32) for v in extra[:4]])
            op = da_prompt(pp3, lam4, extra[4], lam_init)
            os_ = da_sample(ps3, cache[0].reshape(db, past, DA_W), cache[1].reshape(db, past, DA_W),
                            lam4, extra[4], lam_init)
            gate_block = 3
            states.append((
                pp3[:, first:, DA_W:2 * DA_W].reshape(b, n_real, 2 * DA_HEADS, DA_DH),
                pp3[:, first:, 2 * DA_W:3 * DA_W].reshape(b, n_real, DA_HEADS, DA_DV),
                ps3[:, :, DA_W:2 * DA_W].reshape(db, s_len, 2 * DA_HEADS, DA_DH),
                ps3[:, :, 2 * DA_W:3 * DA_W].reshape(db, s_len, DA_HEADS, DA_DV)))
        elif kind == 1:
            cos_p, sin_p = rope_tables(lp, -FRONT_PAD)
            cos_s, sin_s = rope_tables(s_len, past)
            op, sp = ret_prompt(pp3, cos_p, sin_p, extra[0])
            os_, ss = ret_sample(ps3, cos_s, sin_s, extra[0], cache[0])
            gate_block = 2
            states.append((sp, ss))
        else:
            op = sb_prompt(pp3)
            os_ = sb_sample(ps3, cache[0].reshape(db, past, SB_W), cache[1].reshape(db, past, SB_W))
            gate_block = 3
            states.append((
                pp3[:, first:, SB_W:2 * SB_W].reshape(b, n_real, SB_HEADS, SB_DH),
                pp3[:, first:, 2 * SB_W:3 * SB_W].reshape(b, n_real, SB_HEADS, SB_DH),
                ps3[:, :, SB_W:2 * SB_W].reshape(db, s_len, SB_HEADS, SB_DH),
                ps3[:, :, 2 * SB_W:3 * SB_W].reshape(db, s_len, SB_HEADS, SB_DH)))
        wdt = op.shape[-1]
        op2 = op.reshape(b * lp, wdt)
        os2 = os_.reshape(db * s_len, wdt)
        if i + 1 < n_layers:
            xp = post_proj(op2, pp, gate_block, xp, w_out_b)
            xs = post_proj(os2, ps, gate_block, xs, w_out_b)
        else:
            y_sample = post_proj(os2, ps, gate_block, xs, w_out_b, final_g=norm_g_final).reshape(db, s_len, d)
            y_prompt = post_proj(op2, pp, gate_block, xp, w_out_b, batch=b, final_g=norm_g_final,
                                 skip_rows=FRONT_PAD).reshape(b, seq, d)

    k0_p, v0_p, k0_s, v0_s = states[0]
    ret1_p, ret1_s = states[1]
    k2_p, v2_p, k2_s, v2_s = states[2]
    k3_p, v3_p, k3_s, v3_s = states[3]
    return (y_prompt, y_sample, k0_p, v0_p, k0_s, v0_s, ret1_p, ret1_s,
            k2_p, v2_p, k2_s, v2_s, k3_p, v3_p, k3_s, v3_s)
```

```python
import functools
import math

import jax
import jax.numpy as jnp
from jax import lax
from jax.experimental import pallas as pl
from jax.experimental.pallas import tpu as pltpu

F32 = jnp.float32
BF16 = jnp.bfloat16

CHUNK = 64
N_META = 16
NORM_EPS = 1e-6
N_MIXERS = 3
ROPE_BASE = 10000.0

DA_HEADS = 4
DA_DH = 64
DA_DV = 128
DA_W = 512
RET_HEADS = 4
RET_DK = 256
RET_DV = 512
SB_HEADS = 4
SB_DH = 128
SB_W = 512

LANES = 128
BF16_ROWS = 16
FRONT_PAD = 512
ATT_KEYS = 768
ATT_SUB = 256
DA_TQ = 256
SB_TQ = 512
SB_CUM = 128
DA_VROWS = DA_DV + BF16_ROWS
LOG2E = math.log2(math.e)
RET_BLOCK = 256
ROW_TILE = 512
VMEM_LIMIT = 48 * 1024 * 1024

NEG_INF = float("-inf")
NT_DIMS = (((1,), (1,)), ((), ()))
TN_DIMS = (((0,), (0,)), ((), ()))


def _cparams(sem):
    return pltpu.CompilerParams(dimension_semantics=sem, vmem_limit_bytes=VMEM_LIMIT)


def _silu(g):
    return g / (1.0 + jnp.exp(-g))


def _softplus(z):
    return jnp.maximum(z, 0.0) + jnp.log(1.0 + jnp.exp(-jnp.abs(z)))


def _norm_proj_kernel(x_ref, g_ref, w_ref, o_ref):
    x = x_ref[...]
    ms = jnp.mean(x * x, axis=-1, keepdims=True)
    h = (x * lax.rsqrt(ms + NORM_EPS) * g_ref[...]).astype(BF16)
    o_ref[...] = jnp.dot(h, w_ref[...], preferred_element_type=F32)


def norm_proj(x, g, w_bf16, *, col_tile=2048):
    rows, d = x.shape
    e = w_bf16.shape[1]
    tr = min(ROW_TILE, rows)
    te = min(col_tile, e)
    return pl.pallas_call(
        _norm_proj_kernel,
        out_shape=jax.ShapeDtypeStruct((rows, e), F32),
        grid=(e // te, rows // tr),
        in_specs=[
            pl.BlockSpec((tr, d), lambda j, i: (i, 0)),
            pl.BlockSpec((1, d), lambda j, i: (0, 0)),
            pl.BlockSpec((d, te), lambda j, i: (0, j)),
        ],
        out_specs=pl.BlockSpec((tr, te), lambda j, i: (i, j)),
        compiler_params=_cparams(("parallel", "parallel")),
        name="norm_proj",
    )(x, g.reshape(1, d), w_bf16)


def _norm_proj_kv_kernel(x_ref, g_ref, w_ref, o_ref, kb_ref, vt_ref, *, width, heads, ones_rows):
    x = x_ref[...]
    ms = jnp.mean(x * x, axis=-1, keepdims=True)
    h = (x * lax.rsqrt(ms + NORM_EPS) * g_ref[...]).astype(BF16)
    p = jnp.dot(h, w_ref[...], preferred_element_type=F32)
    o_ref[...] = p
    kb_ref[...] = p[:, width:2 * width].astype(BF16)
    vt = p[:, 2 * width:3 * width].T.astype(BF16)
    if ones_rows:
        dv = width // heads
        for hd in range(heads):
            base = hd * (dv + ones_rows)
            vt_ref[base:base + dv, :] = vt[hd * dv:(hd + 1) * dv, :]
            vt_ref[base + dv:base + dv + ones_rows, :] = jnp.ones((ones_rows, vt.shape[1]), BF16)
    else:
        vt_ref[...] = vt


def norm_proj_kv(x, g, w_bf16, batch, width, heads, ones_rows):
    rows, d = x.shape
    e = w_bf16.shape[1]
    per = rows // batch
    tr = min(ROW_TILE, per)
    n_per = per // tr
    vrows = width + heads * ones_rows
    return pl.pallas_call(
        functools.partial(_norm_proj_kv_kernel, width=width, heads=heads, ones_rows=ones_rows),
        out_shape=(jax.ShapeDtypeStruct((rows, e), F32),
                   jax.ShapeDtypeStruct((rows, width), BF16),
                   jax.ShapeDtypeStruct((batch, vrows, per), BF16)),
        grid=(rows // tr,),
        in_specs=[
            pl.BlockSpec((tr, d), lambda i: (i, 0)),
            pl.BlockSpec((1, d), lambda i: (0, 0)),
            pl.BlockSpec((d, e), lambda i: (0, 0)),
        ],
        out_specs=(pl.BlockSpec((tr, e), lambda i: (i, 0)),
                   pl.BlockSpec((tr, width), lambda i: (i, 0)),
                   pl.BlockSpec((None, vrows, tr), lambda i: (i // n_per, 0, i % n_per))),
        compiler_params=_cparams(("parallel",)),
        name="norm_proj_kv",
    )(x, g.reshape(1, d), w_bf16)


def _post_kernel(o_ref, gate_ref, x_ref, w_ref, out_ref):
    a = (o_ref[...] * _silu(gate_ref[...])).astype(BF16)
    out_ref[...] = x_ref[...] + jnp.dot(a, w_ref[...], preferred_element_type=F32)


def _post_final_kernel(o_ref, gate_ref, x_ref, w_ref, gf_ref, out_ref):
    a = (o_ref[...] * _silu(gate_ref[...])).astype(BF16)
    xn = x_ref[...] + jnp.dot(a, w_ref[...], preferred_element_type=F32)
    ms = jnp.mean(xn * xn, axis=-1, keepdims=True)
    out_ref[...] = xn * lax.rsqrt(ms + NORM_EPS) * gf_ref[...]


def post_proj(o, proj, gate_block, x, w_bf16, *, batch=1, final_g=None, skip_rows=0):
    rows, wdt = o.shape
    d = x.shape[1]
    per = rows // batch
    tr = min(ROW_TILE, per)
    off = skip_rows // tr
    n_in = per // tr
    n = n_in - off
    in_specs = [
        pl.BlockSpec((tr, wdt), lambda bb, i: (bb * n_in + off + i, 0)),
        pl.BlockSpec((tr, wdt), lambda bb, i: (bb * n_in + off + i, gate_block)),
        pl.BlockSpec((tr, d), lambda bb, i: (bb * n_in + off + i, 0)),
        pl.BlockSpec((wdt, d), lambda bb, i: (0, 0)),
    ]
    args = [o, proj, x, w_bf16]
    kern = _post_kernel
    if final_g is not None:
        in_specs.append(pl.BlockSpec((1, d), lambda bb, i: (0, 0)))
        args.append(final_g.reshape(1, d))
        kern = _post_final_kernel
    return pl.pallas_call(
        kern,
        out_shape=jax.ShapeDtypeStruct((batch * n * tr, d), F32),
        grid=(batch, n),
        in_specs=in_specs,
        out_specs=pl.BlockSpec((tr, d), lambda bb, i: (bb * n + i, 0)),
        compiler_params=_cparams(("parallel", "parallel")),
        name="post_proj",
    )(*args)


def _rope_kernel(inv_ref, cos_ref, sin_ref, *, start):
    tr = cos_ref.shape[0]
    row = pl.program_id(0) * tr + lax.broadcasted_iota(jnp.int32, (tr, LANES), 0)
    ang = (row + start).astype(F32) * inv_ref[...]
    cos_ref[...] = jnp.cos(ang)
    sin_ref[...] = jnp.sin(ang)


def rope_tables(n, start):
    half = RET_DK // 2
    inv = (ROPE_BASE ** (-jnp.linspace(0.0, 1.0, half, dtype=F32))).reshape(1, half)
    tr = min(ROW_TILE, n)
    return pl.pallas_call(
        functools.partial(_rope_kernel, start=start),
        out_shape=(jax.ShapeDtypeStruct((n, half), F32), jax.ShapeDtypeStruct((n, half), F32)),
        grid=(n // tr,),
        in_specs=[pl.BlockSpec((1, half), lambda i: (0, 0))],
        out_specs=(pl.BlockSpec((tr, half), lambda i: (i, 0)), pl.BlockSpec((tr, half), lambda i: (i, 0))),
        compiler_params=_cparams(("parallel",)),
        name="rope_tables",
    )(inv)


def _rotary(x, cos, sin):
    half = RET_DK // 2
    x1, x2 = x[:, :half], x[:, half:]
    return jnp.concatenate([x1 * cos - x2 * sin, x1 * sin + x2 * cos], axis=1)


def _da_split_q(q):
    rows = q.shape[0]
    lane = lax.broadcasted_iota(jnp.int32, (rows, LANES), 1)
    out = []
    for h in range(DA_HEADS):
        qh = q[:, h * LANES:(h + 1) * LANES]
        qa = jnp.where(lane < DA_DH, qh, 0.0)
        qb = jnp.where(lane >= DA_DH, qh, 0.0)
        out.append(jnp.concatenate([qa, qb], axis=0).astype(BF16))
    return out


def _da_update(h, qh, kh, vh, vis, m_s, l_s, acc_s):
    s = lax.dot_general(qh, kh, NT_DIMS, preferred_element_type=F32)
    if vis is not None:
        s = jnp.where(vis, s, NEG_INF)
    m_prev = m_s[h]
    m_new = jnp.maximum(m_prev, jnp.max(s, axis=1, keepdims=True))
    m_use = jnp.where(m_new == NEG_INF, 0.0, m_new)
    alpha = jnp.exp(m_prev - m_use)
    e = jnp.exp(s - m_use)
    l_s[h] = alpha * l_s[h] + jnp.sum(e, axis=1, keepdims=True)
    acc_s[h] = alpha * acc_s[h] + jnp.dot(e.astype(BF16), vh, preferred_element_type=F32)
    m_s[h] = m_new


def _da_finish(lam_ref, g_ref, o_ref, l_s, acc_s, rows, lam_init):
    lam = (jnp.exp(jnp.sum(lam_ref[0:1, :] * lam_ref[1:2, :], axis=1, keepdims=True))
           - jnp.exp(jnp.sum(lam_ref[2:3, :] * lam_ref[3:4, :], axis=1, keepdims=True)) + lam_init)
    for h in range(DA_HEADS):
        l = l_s[h]
        o = acc_s[h] / jnp.where(l == 0.0, 1.0, l)
        d = o[:rows] - lam * o[rows:]
        ms = jnp.mean(d * d, axis=1, keepdims=True)
        o_ref[:, h * DA_DV:(h + 1) * DA_DV] = d * lax.rsqrt(ms + NORM_EPS) * g_ref[...] * (1.0 - lam_init)


def _da_prompt_kernel(qi_ref, kj_ref, q_ref, k_ref, vt_ref, lam_ref, g_ref, o_ref,
                      qs, m_s, acc_s, *, tq, lam_init):
    step = pl.program_id(1)
    qi = qi_ref[step]
    kj = kj_ref[step]
    last = (qi * tq + tq - 1) // ATT_KEYS
    on_diag = (kj + 1) * ATT_KEYS > qi * tq
    n_sub = ATT_KEYS // ATT_SUB

    @pl.when(kj == 0)
    def _():
        q = q_ref[...] * (DA_DH ** -0.5 * LOG2E)
        row = lax.broadcasted_iota(jnp.int32, (LANES, tq), 0)
        for h in range(DA_HEADS):
            qt = q[:, h * LANES:(h + 1) * LANES].T
            qa = jnp.where(row < DA_DH, qt, 0.0)
            qb = jnp.where(row >= DA_DH, qt, 0.0)
            qs[h] = jnp.concatenate([qa, qb], axis=1).astype(BF16)
        m_s[...] = jnp.full(m_s.shape, NEG_INF, F32)
        acc_s[...] = jnp.zeros(acc_s.shape, F32)

    first = FRONT_PAD - N_META

    def run(mode):
        subs = range(n_sub)
        vis = {}
        if mode == "pad":
            subs = [sub for sub in subs if (sub + 1) * ATT_SUB > first]
            for sub in subs:
                if sub * ATT_SUB < first:
                    pk = sub * ATT_SUB + lax.broadcasted_iota(jnp.int32, (ATT_SUB, 2 * tq), 0)
                    vis[sub] = pk >= first
        elif mode == "diag":
            pk0 = kj * ATT_KEYS + lax.broadcasted_iota(jnp.int32, (ATT_SUB, 2 * tq), 0)
            pq = qi * tq + (lax.broadcasted_iota(jnp.int32, (ATT_SUB, 2 * tq), 1) & (tq - 1))
            for sub in subs:
                pk = pk0 + sub * ATT_SUB
                vis[sub] = (pk >= first) & ((pk >> 6) <= (pq >> 6))
        items = [(sub, h) for sub in subs for h in range(DA_HEADS)]

        def scores(item):
            sub, h = item
            return jnp.dot(k_ref[sub * ATT_SUB:(sub + 1) * ATT_SUB, h * LANES:(h + 1) * LANES], qs[h],
                           preferred_element_type=F32)

        def softmax(item, s):
            sub, h = item
            if sub in vis:
                s = jnp.where(vis[sub], s, NEG_INF)
            m_prev = m_s[h]
            m_new = jnp.maximum(m_prev, jnp.max(s, axis=0, keepdims=True))
            m_use = jnp.where(m_new == NEG_INF, 0.0, m_new)
            m_s[h] = m_new
            return jnp.exp2(m_prev - m_use), jnp.exp2(s - m_use).astype(BF16)

        def accumulate(item, alpha, e):
            sub, h = item
            vt = vt_ref[h * DA_VROWS:(h + 1) * DA_VROWS, sub * ATT_SUB:(sub + 1) * ATT_SUB]
            acc_s[h] = alpha * acc_s[h] + jnp.dot(vt, e, preferred_element_type=F32)

        n = len(items)
        s_live, e_live = {}, {}
        for t in range(n + 2):
            if t < n:
                s_live[t] = scores(items[t])
            if t >= 2:
                accumulate(items[t - 2], *e_live.pop(t - 2))
            if 1 <= t <= n:
                e_live[t - 1] = softmax(items[t - 1], s_live.pop(t - 1))

    off_diag = jnp.logical_not(on_diag)
    pl.when(on_diag)(lambda: run("diag"))
    pl.when((kj == 0) & off_diag)(lambda: run("pad"))
    pl.when((kj != 0) & off_diag)(lambda: run("open"))

    @pl.when(kj == last)
    def _():
        lam = (jnp.exp(jnp.sum(lam_ref[0:1, :] * lam_ref[1:2, :], axis=1, keepdims=True))
               - jnp.exp(jnp.sum(lam_ref[2:3, :] * lam_ref[3:4, :], axis=1, keepdims=True)) + lam_init)
        for h in range(DA_HEADS):
            acc = acc_s[h]
            l = acc[DA_DV:DA_DV + 1, :]
            o = acc[:DA_DV, :] / jnp.where(l == 0.0, 1.0, l)
            d = o[:, :tq] - lam * o[:, tq:]
            ms = jnp.mean(d * d, axis=0, keepdims=True)
            y = d * lax.rsqrt(ms + NORM_EPS) * g_ref[...] * (1.0 - lam_init)
            o_ref[:, h * DA_DV:(h + 1) * DA_DV] = y.T


def _tri_steps(nq, tq, descending):
    qi, kj = [], []
    for i in range(nq):
        last = (i * tq + tq - 1) // ATT_KEYS
        ks = range(last, -1, -1) if descending else range(last + 1)
        for k in ks:
            qi.append(i)
            kj.append(k)
    return jnp.asarray(qi, jnp.int32), jnp.asarray(kj, jnp.int32)


def da_prompt(proj, kb, vt, lam4, subln_g, lam_init):
    b, lp, _ = proj.shape
    tq = DA_TQ
    qi, kj = _tri_steps(lp // tq, tq, descending=False)
    kern = functools.partial(_da_prompt_kernel, tq=tq, lam_init=lam_init)
    grid_spec = pltpu.PrefetchScalarGridSpec(
        num_scalar_prefetch=2,
        grid=(b, qi.shape[0]),
        in_specs=[
            pl.BlockSpec((None, tq, DA_W), lambda bb, s, qi, kj: (bb, qi[s], 0)),
            pl.BlockSpec((None, ATT_KEYS, DA_W), lambda bb, s, qi, kj: (bb, kj[s], 0)),
            pl.BlockSpec((None, DA_HEADS * DA_VROWS, ATT_KEYS), lambda bb, s, qi, kj: (bb, 0, kj[s])),
            pl.BlockSpec((4, DA_DH), lambda bb, s, qi, kj: (0, 0)),
            pl.BlockSpec((DA_DV, 1), lambda bb, s, qi, kj: (0, 0)),
        ],
        out_specs=pl.BlockSpec((None, tq, DA_W), lambda bb, s, qi, kj: (bb, qi[s], 0)),
        scratch_shapes=[
            pltpu.VMEM((DA_HEADS, LANES, 2 * tq), BF16),
            pltpu.VMEM((DA_HEADS, 1, 2 * tq), F32),
            pltpu.VMEM((DA_HEADS, DA_VROWS, 2 * tq), F32),
        ],
    )
    return pl.pallas_call(
        kern,
        out_shape=jax.ShapeDtypeStruct((b, lp, DA_W), F32),
        grid_spec=grid_spec,
        compiler_params=_cparams(("parallel", "arbitrary")),
        name="da_prompt",
    )(qi, kj, proj, kb, vt, lam4, subln_g.reshape(DA_DV, 1))


def _da_sample_kernel(q_ref, kn_ref, vn_ref, kc_ref, vc_ref, lam_ref, g_ref, o_ref,
                      qs, m_s, l_s, acc_s, *, rows, lam_init):
    j = pl.program_id(1)

    @pl.when(j == 0)
    def _():
        for h, qh in enumerate(_da_split_q(q_ref[...] * (DA_DH ** -0.5))):
            qs[h] = qh
        m_s[...] = jnp.full(m_s.shape, NEG_INF, F32)
        l_s[...] = jnp.zeros(l_s.shape, F32)
        acc_s[...] = jnp.zeros(acc_s.shape, F32)

    for h in range(DA_HEADS):
        kh = kc_ref[:, h * LANES:(h + 1) * LANES].astype(BF16)
        vh = vc_ref[:, h * DA_DV:(h + 1) * DA_DV].astype(BF16)
        _da_update(h, qs[h], kh, vh, None, m_s, l_s, acc_s)

    @pl.when(j == pl.num_programs(1) - 1)
    def _():
        for h in range(DA_HEADS):
            kh = kn_ref[:, h * LANES:(h + 1) * LANES].astype(BF16)
            vh = vn_ref[:, h * DA_DV:(h + 1) * DA_DV].astype(BF16)
            _da_update(h, qs[h], kh, vh, None, m_s, l_s, acc_s)
        _da_finish(lam_ref, g_ref, o_ref, l_s, acc_s, rows, lam_init)


def da_sample(proj, cache_k, cache_v, lam4, subln_g, lam_init):
    db, s, _ = proj.shape
    past = cache_k.shape[1]
    tk = min(1024, past)
    kern = functools.partial(_da_sample_kernel, rows=s, lam_init=lam_init)
    return pl.pallas_call(
        kern,
        out_shape=jax.ShapeDtypeStruct((db, s, DA_W), F32),
        grid=(db, past // tk),
        in_specs=[
            pl.BlockSpec((None, s, DA_W), lambda b, j: (b, 0, 0)),
            pl.BlockSpec((None, s, DA_W), lambda b, j: (b, 0, 1)),
            pl.BlockSpec((None, s, DA_W), lambda b, j: (b, 0, 2)),
            pl.BlockSpec((None, tk, DA_W), lambda b, j: (b, j, 0)),
            pl.BlockSpec((None, tk, DA_W), lambda b, j: (b, j, 0)),
            pl.BlockSpec((4, DA_DH), lambda b, j: (0, 0)),
            pl.BlockSpec((1, DA_DV), lambda b, j: (0, 0)),
        ],
        out_specs=pl.BlockSpec((None, s, DA_W), lambda b, j: (b, 0, 0)),
        scratch_shapes=[
            pltpu.VMEM((DA_HEADS, 2 * s, LANES), BF16),
            pltpu.VMEM((DA_HEADS, 2 * s, 1), F32),
            pltpu.VMEM((DA_HEADS, 2 * s, 1), F32),
            pltpu.VMEM((DA_HEADS, 2 * s, DA_DV), F32),
        ],
        compiler_params=_cparams(("parallel", "arbitrary")),
        name="da_sample",
    )(proj, proj, proj, cache_k, cache_v, lam4, subln_g.reshape(1, DA_DV))


def _sb_update(h, qh, kh, vh, u, vis, r_s, acc_s):
    z = lax.dot_general(qh, kh, NT_DIMS, preferred_element_type=F32)
    lk = -_softplus(z)
    if vis is not None:
        lk = jnp.where(vis, lk, 0.0)
    hi = lk.astype(BF16)
    lo = (lk - hi.astype(F32)).astype(BF16)
    tail = (jnp.dot(hi, u, preferred_element_type=F32) + jnp.dot(lo, u, preferred_element_type=F32)) + r_s[h]
    a = jnp.exp(z + tail)
    if vis is not None:
        a = jnp.where(vis, a, 0.0)
    acc_s[h] = acc_s[h] + jnp.dot(a.astype(BF16), vh, preferred_element_type=F32)
    r_s[h] = r_s[h] + jnp.sum(lk, axis=1, keepdims=True)


def _sb_prompt_kernel(qi_ref, kj_ref, q_ref, k_ref, vt_ref, w_ref, o_ref, qs, r_s, acc_s, *, tq):
    step = pl.program_id(1)
    qi = qi_ref[step]
    kj = kj_ref[step]
    last = (qi * tq + tq - 1) // ATT_KEYS
    on_diag = (kj + 1) * ATT_KEYS > qi * tq
    n_sub = ATT_KEYS // ATT_SUB

    @pl.when(kj == last)
    def _():
        q = q_ref[...] * (SB_DH ** -0.5 * LOG2E)
        for h in range(SB_HEADS):
            qs[h] = q[:, h * SB_DH:(h + 1) * SB_DH].T.astype(BF16)
        r_s[...] = jnp.zeros(r_s.shape, F32)
        acc_s[...] = jnp.zeros(acc_s.shape, F32)

    def run(masked):
        items = [(sub, h) for sub in range(n_sub - 1, -1, -1) for h in range(SB_HEADS)]
        if masked:
            pk0 = kj * ATT_KEYS + lax.broadcasted_iota(jnp.int32, (ATT_SUB, tq), 0)
            pq = qi * tq + lax.broadcasted_iota(jnp.int32, (ATT_SUB, tq), 1)
            vis = [pk0 + sub * ATT_SUB < pq for sub in range(n_sub)]
        w2 = w_ref[...]
        n_blk = ATT_SUB // SB_CUM

        def logits(item):
            sub, h = item
            return jnp.dot(k_ref[sub * ATT_SUB:(sub + 1) * ATT_SUB, h * SB_DH:(h + 1) * SB_DH], qs[h],
                           preferred_element_type=F32)

        def split(item, z):
            sub, _ = item
            bits = lax.bitcast_convert_type(z, jnp.int32)
            neg_abs = lax.bitcast_convert_type(bits | jnp.int32(-2 ** 31), F32)
            sp = jnp.maximum(z, 0.0) + jnp.log(1.0 + jnp.exp2(neg_abs)) * LOG2E
            if masked:
                sp = jnp.where(vis[sub], sp, 0.0)
            hi = lax.bitcast_convert_type(lax.bitcast_convert_type(sp, jnp.uint32) & jnp.uint32(0xFFFF0000), F32)
            return hi.astype(BF16), (sp - hi).astype(BF16)

        def suffix(hi, lo):
            return [jnp.dot(w2, jnp.concatenate([hi[b * SB_CUM:(b + 1) * SB_CUM], lo[b * SB_CUM:(b + 1) * SB_CUM]],
                                                axis=0), preferred_element_type=F32) for b in range(n_blk)]

        def weights(item, z, tins):
            sub, h = item
            r = r_s[h]
            tails = [None] * n_blk
            for b in range(n_blk - 1, -1, -1):
                tails[b] = tins[b] + r
                r = r + tins[b][0:1, :]
            r_s[h] = r
            a = jnp.exp2(z - jnp.concatenate(tails, axis=0))
            if masked:
                a = jnp.where(vis[sub], a, 0.0)
            return a.astype(BF16)

        def accumulate(item, a):
            sub, h = item
            vt = vt_ref[h * SB_DH:(h + 1) * SB_DH, sub * ATT_SUB:(sub + 1) * ATT_SUB]
            acc_s[h] = acc_s[h] + jnp.dot(vt, a, preferred_element_type=F32)

        n = len(items)
        z_live, hl_live, t_live, a_live = {}, {}, {}, {}
        for t in range(n + 4):
            if t < n:
                z_live[t] = logits(items[t])
            if 2 <= t < n + 2:
                t_live[t - 2] = suffix(*hl_live.pop(t - 2))
            if t >= 4:
                accumulate(items[t - 4], a_live.pop(t - 4))
            if 1 <= t < n + 1:
                hl_live[t - 1] = split(items[t - 1], z_live[t - 1])
            if 3 <= t < n + 3:
                a_live[t - 3] = weights(items[t - 3], z_live.pop(t - 3), t_live.pop(t - 3))

    pl.when(on_diag)(lambda: run(True))
    pl.when(jnp.logical_not(on_diag))(lambda: run(False))

    @pl.when(kj == 0)
    def _():
        for h in range(SB_HEADS):
            o_ref[:, h * SB_DH:(h + 1) * SB_DH] = acc_s[h].T


def _tri_ge(n, transpose=False):
    idx = jnp.arange(n)
    m = idx[:, None] >= idx[None, :]
    return (m.T if transpose else m).astype(BF16)


def sb_prompt(proj, kb, vt):
    b, lp, _ = proj.shape
    tq = SB_TQ
    qi, kj = _tri_steps(lp // tq, tq, descending=True)
    kern = functools.partial(_sb_prompt_kernel, tq=tq)
    grid_spec = pltpu.PrefetchScalarGridSpec(
        num_scalar_prefetch=2,
        grid=(b, qi.shape[0]),
        in_specs=[
            pl.BlockSpec((None, tq, SB_W), lambda bb, s, qi, kj: (bb, qi[s], 0)),
            pl.BlockSpec((None, ATT_KEYS, SB_W), lambda bb, s, qi, kj: (bb, kj[s], 0)),
            pl.BlockSpec((None, SB_W, ATT_KEYS), lambda bb, s, qi, kj: (bb, 0, kj[s])),
            pl.BlockSpec((SB_CUM, 2 * SB_CUM), lambda bb, s, qi, kj: (0, 0)),
        ],
        out_specs=pl.BlockSpec((None, tq, SB_W), lambda bb, s, qi, kj: (bb, qi[s], 0)),
        scratch_shapes=[
            pltpu.VMEM((SB_HEADS, SB_DH, tq), BF16),
            pltpu.VMEM((SB_HEADS, 1, tq), F32),
            pltpu.VMEM((SB_HEADS, SB_DH, tq), F32),
        ],
    )
    return pl.pallas_call(
        kern,
        out_shape=jax.ShapeDtypeStruct((b, lp, SB_W), F32),
        grid_spec=grid_spec,
        compiler_params=_cparams(("parallel", "arbitrary")),
        name="sb_prompt",
    )(qi, kj, proj, kb, vt, jnp.tile(_tri_ge(SB_CUM, transpose=True), (1, 2)))


def _sb_sample_kernel(q_ref, kn_ref, vn_ref, kc_ref, vc_ref, un_ref, uc_ref, o_ref, qs, r_s, acc_s, *, rows):
    j = pl.program_id(1)

    @pl.when(j == 0)
    def _():
        q = q_ref[...] * (SB_DH ** -0.5)
        for h in range(SB_HEADS):
            qs[h] = q[:, h * SB_DH:(h + 1) * SB_DH].astype(BF16)
        r_s[...] = jnp.zeros(r_s.shape, F32)
        acc_s[...] = jnp.zeros(acc_s.shape, F32)
        iq = lax.broadcasted_iota(jnp.int32, (rows, rows), 0)
        ik = lax.broadcasted_iota(jnp.int32, (rows, rows), 1)
        vis = ik < iq
        un = un_ref[...]
        for h in range(SB_HEADS):
            kh = kn_ref[:, h * SB_DH:(h + 1) * SB_DH].astype(BF16)
            vh = vn_ref[:, h * SB_DH:(h + 1) * SB_DH].astype(BF16)
            _sb_update(h, qs[h], kh, vh, un, vis, r_s, acc_s)

    uc = uc_ref[...]
    for h in range(SB_HEADS):
        kh = kc_ref[:, h * SB_DH:(h + 1) * SB_DH].astype(BF16)
        vh = vc_ref[:, h * SB_DH:(h + 1) * SB_DH].astype(BF16)
        _sb_update(h, qs[h], kh, vh, uc, None, r_s, acc_s)

    @pl.when(j == pl.num_programs(1) - 1)
    def _():
        for h in range(SB_HEADS):
            o_ref[:, h * SB_DH:(h + 1) * SB_DH] = acc_s[h]


def sb_sample(proj, cache_k, cache_v):
    db, s, _ = proj.shape
    past = cache_k.shape[1]
    tk = min(256, past)
    n = past // tk
    kern = functools.partial(_sb_sample_kernel, rows=s)
    return pl.pallas_call(
        kern,
        out_shape=jax.ShapeDtypeStruct((db, s, SB_W), F32),
        grid=(db, n),
        in_specs=[
            pl.BlockSpec((None, s, SB_W), lambda b, j: (b, 0, 0)),
            pl.BlockSpec((None, s, SB_W), lambda b, j: (b, 0, 1)),
            pl.BlockSpec((None, s, SB_W), lambda b, j: (b, 0, 2)),
            pl.BlockSpec((None, tk, SB_W), lambda b, j: (b, n - 1 - j, 0)),
            pl.BlockSpec((None, tk, SB_W), lambda b, j: (b, n - 1 - j, 0)),
            pl.BlockSpec((s, s), lambda b, j: (0, 0)),
            pl.BlockSpec((tk, tk), lambda b, j: (0, 0)),
        ],
        out_specs=pl.BlockSpec((None, s, SB_W), lambda b, j: (b, 0, 0)),
        scratch_shapes=[
            pltpu.VMEM((SB_HEADS, s, SB_DH), BF16),
            pltpu.VMEM((SB_HEADS, s, 1), F32),
            pltpu.VMEM((SB_HEADS, s, SB_DH), F32),
        ],
        compiler_params=_cparams(("parallel", "arbitrary")),
        name="sb_sample",
    )(proj, proj, proj, cache_k, cache_v, _tri_ge(s), _tri_ge(tk))


def _ret_block(q, k, v, cos, sin, state, log_gamma):
    c = q.shape[0]
    qr = _rotary(q, cos, sin)
    kr = _rotary(k, cos, sin) * (RET_DK ** -0.5)
    ii = lax.broadcasted_iota(jnp.int32, (c, c), 0)
    jj = lax.broadcasted_iota(jnp.int32, (c, c), 1)
    rel = (ii - jj).astype(F32)
    decay = jnp.where(rel >= 0.0, jnp.exp(jnp.maximum(rel, 0.0) * log_gamma), 0.0)
    idx = lax.broadcasted_iota(jnp.int32, (c, 1), 0).astype(F32)
    q_dec = (qr * jnp.exp((idx + 1.0) * log_gamma)).astype(BF16)
    k_dec = (kr * jnp.exp((c - 1.0 - idx) * log_gamma)).astype(BF16)
    vb = v.astype(BF16)
    scores = lax.dot_general(qr.astype(BF16), kr.astype(BF16), NT_DIMS, preferred_element_type=F32) * decay
    o = (jnp.dot(scores.astype(BF16), vb, preferred_element_type=F32)
         + jnp.dot(q_dec, state.astype(BF16), preferred_element_type=F32))
    new_state = (jnp.exp(c * log_gamma) * state
                 + lax.dot_general(k_dec, vb, TN_DIMS, preferred_element_type=F32))
    return o, new_state


def _ret_norm(o, g):
    ms = jnp.mean(o * o, axis=1, keepdims=True)
    return o * lax.rsqrt(ms + NORM_EPS) * g


def _log_gamma(h):
    hv = jnp.zeros((1, 1), F32) + h.astype(F32)
    return jnp.log(1.0 - jnp.exp2(-5.0 - hv))


def _ret_prompt_kernel(q_ref, k_ref, v_ref, cos_ref, sin_ref, g_ref, o_ref, sf_ref, st):
    j = pl.program_id(2)

    @pl.when(j == 0)
    def _():
        st[...] = jnp.zeros(st.shape, F32)

    o, new_state = _ret_block(q_ref[...], k_ref[...], v_ref[...], cos_ref[...], sin_ref[...], st[...],
                              _log_gamma(pl.program_id(1)))
    o_ref[...] = _ret_norm(o, g_ref[...])
    st[...] = new_state

    @pl.when(j == pl.num_programs(2) - 1)
    def _():
        sf_ref[...] = new_state


def ret_prompt(proj, cos, sin, gn_g):
    b, lp, _ = proj.shape
    c = RET_BLOCK
    kv_off = (RET_HEADS * RET_DK) // RET_DK
    v_off = (2 * RET_HEADS * RET_DK) // RET_DV
    return pl.pallas_call(
        _ret_prompt_kernel,
        out_shape=(jax.ShapeDtypeStruct((b, lp, RET_HEADS * RET_DV), F32),
                   jax.ShapeDtypeStruct((b, RET_HEADS, RET_DK, RET_DV), F32)),
        grid=(b, RET_HEADS, lp // c),
        in_specs=[
            pl.BlockSpec((None, c, RET_DK), lambda bb, h, j: (bb, j, h)),
            pl.BlockSpec((None, c, RET_DK), lambda bb, h, j: (bb, j, kv_off + h)),
            pl.BlockSpec((None, c, RET_DV), lambda bb, h, j: (bb, j, v_off + h)),
            pl.BlockSpec((c, RET_DK // 2), lambda bb, h, j: (j, 0)),
            pl.BlockSpec((c, RET_DK // 2), lambda bb, h, j: (j, 0)),
            pl.BlockSpec((None, 1, RET_DV), lambda bb, h, j: (h, 0, 0)),
        ],
        out_specs=(pl.BlockSpec((None, c, RET_DV), lambda bb, h, j: (bb, j, h)),
                   pl.BlockSpec((None, None, RET_DK, RET_DV), lambda bb, h, j: (bb, h, 0, 0))),
        scratch_shapes=[pltpu.VMEM((RET_DK, RET_DV), F32)],
        compiler_params=_cparams(("parallel", "parallel", "arbitrary")),
        name="ret_prompt",
    )(proj, proj, proj, cos, sin, gn_g.reshape(RET_HEADS, 1, RET_DV))


def _ret_sample_kernel(q_ref, k_ref, v_ref, cos_ref, sin_ref, g_ref, s_ref, o_ref, sf_ref):
    o, new_state = _ret_block(q_ref[...], k_ref[...], v_ref[...], cos_ref[...], sin_ref[...], s_ref[...],
                              _log_gamma(pl.program_id(1)))
    o_ref[...] = _ret_norm(o, g_ref[...])
    sf_ref[...] = new_state


def ret_sample(proj, cos, sin, gn_g, state):
    db, s, _ = proj.shape
    kv_off = (RET_HEADS * RET_DK) // RET_DK
    v_off = (2 * RET_HEADS * RET_DK) // RET_DV
    return pl.pallas_call(
        _ret_sample_kernel,
        out_shape=(jax.ShapeDtypeStruct((db, s, RET_HEADS * RET_DV), F32),
                   jax.ShapeDtypeStruct(state.shape, F32)),
        grid=(db, RET_HEADS),
        in_specs=[
            pl.BlockSpec((None, s, RET_DK), lambda b, h: (b, 0, h)),
            pl.BlockSpec((None, s, RET_DK), lambda b, h: (b, 0, kv_off + h)),
            pl.BlockSpec((None, s, RET_DV), lambda b, h: (b, 0, v_off + h)),
            pl.BlockSpec((s, RET_DK // 2), lambda b, h: (0, 0)),
            pl.BlockSpec((s, RET_DK // 2), lambda b, h: (0, 0)),
            pl.BlockSpec((None, 1, RET_DV), lambda b, h: (h, 0, 0)),
            pl.BlockSpec((None, None, RET_DK, RET_DV), lambda b, h: (b, h, 0, 0)),
        ],
        out_specs=(pl.BlockSpec((None, s, RET_DV), lambda b, h: (b, 0, h)),
                   pl.BlockSpec((None, None, RET_DK, RET_DV), lambda b, h: (b, h, 0, 0))),
        compiler_params=_cparams(("parallel", "parallel")),
        name="ret_sample",
    )(proj, proj, proj, cos, sin, gn_g.reshape(RET_HEADS, 1, RET_DV), state)


def kernel(x_prompt, x_sample, cache_k_l0, cache_v_l0, state_ret_l1, cache_k_l2, cache_v_l2, cache_k_l3, cache_v_l3, meta_tokens, norm_g_0, w_in_0, lam_q1_0, lam_k1_0, lam_q2_0, lam_k2_0, subln_g_0, w_out_0, norm_g_1, w_in_1, gn_g_1, w_out_1, norm_g_2, w_in_2, w_out_2, norm_g_3, w_in_3, lam_q1_3, lam_k1_3, lam_q2_3, lam_k2_3, subln_g_3, w_out_3, norm_g_final):
    b, seq, d = x_prompt.shape
    db, s_len, _ = x_sample.shape
    past = cache_k_l0.shape[1]
    assert seq % FRONT_PAD == 0 and N_META <= FRONT_PAD
    lp = FRONT_PAD + seq
    n_real = N_META + seq
    first = FRONT_PAD - N_META

    meta = jnp.broadcast_to(meta_tokens.astype(F32)[None], (b, N_META, d))
    xp = jnp.concatenate([jnp.zeros((b, first, d), F32), meta, x_prompt], axis=1).reshape(b * lp, d)
    xs = x_sample.reshape(db * s_len, d)

    layers = [
        (norm_g_0, w_in_0, w_out_0, (lam_q1_0, lam_k1_0, lam_q2_0, lam_k2_0, subln_g_0), (cache_k_l0, cache_v_l0)),
        (norm_g_1, w_in_1, w_out_1, (gn_g_1,), (state_ret_l1,)),
        (norm_g_2, w_in_2, w_out_2, (), (cache_k_l2, cache_v_l2)),
        (norm_g_3, w_in_3, w_out_3, (lam_q1_3, lam_k1_3, lam_q2_3, lam_k2_3, subln_g_3), (cache_k_l3, cache_v_l3)),
    ]
    states = []
    y_prompt = y_sample = None
    n_layers = len(layers)
    for i, (norm_g, w_in, w_out, extra, cache) in enumerate(layers):
        kind = i % N_MIXERS
        w_in_b = w_in.astype(BF16)
        w_out_b = w_out.astype(BF16)
        if kind == 1:
            pp = norm_proj(xp, norm_g, w_in_b)
        else:
            if kind == 0:
                pp, kb, vt = norm_proj_kv(xp, norm_g, w_in_b, b, DA_W, DA_HEADS, BF16_ROWS)
            else:
                pp, kb, vt = norm_proj_kv(xp, norm_g, w_in_b, b, SB_W, SB_HEADS, 0)
            kb = kb.reshape(b, lp, kb.shape[-1])
        ps = norm_proj(xs, norm_g, w_in_b)
        e = w_in.shape[1]
        pp3 = pp.reshape(b, lp, e)
        ps3 = ps.reshape(db, s_len, e)
        if kind == 0:
            lam_init = 0.8 - 0.6 * math.exp(-0.3 * i)
            lam4 = jnp.stack([v.astype(F32) for v in extra[:4]])
            op = da_prompt(pp3, kb, vt, lam4, extra[4], lam_init)
            os_ = da_sample(ps3, cache[0].reshape(db, past, DA_W), cache[1].reshape(db, past, DA_W),
                            lam4, extra[4], lam_init)
            gate_block = 3
            states.append((
                pp3[:, first:, DA_W:2 * DA_W].reshape(b, n_real, 2 * DA_HEADS, DA_DH),
                pp3[:, first:, 2 * DA_W:3 * DA_W].reshape(b, n_real, DA_HEADS, DA_DV),
                ps3[:, :, DA_W:2 * DA_W].reshape(db, s_len, 2 * DA_HEADS, DA_DH),
                ps3[:, :, 2 * DA_W:3 * DA_W].reshape(db, s_len, DA_HEADS, DA_DV)))
        elif kind == 1:
            cos_p, sin_p = rope_tables(lp, -FRONT_PAD)
            cos_s, sin_s = rope_tables(s_len, past)
            op, sp = ret_prompt(pp3, cos_p, sin_p, extra[0])
            os_, ss = ret_sample(ps3, cos_s, sin_s, extra[0], cache[0])
            gate_block = 2
            states.append((sp, ss))
        else:
            op = sb_prompt(pp3, kb, vt)
            os_ = sb_sample(ps3, cache[0].reshape(db, past, SB_W), cache[1].reshape(db, past, SB_W))
            gate_block = 3
            states.append((
                pp3[:, first:, SB_W:2 * SB_W].reshape(b, n_real, SB_HEADS, SB_DH),
                pp3[:, first:, 2 * SB_W:3 * SB_W].reshape(b, n_real, SB_HEADS, SB_DH),
                ps3[:, :, SB_W:2 * SB_W].reshape(db, s_len, SB_HEADS, SB_DH),
                ps3[:, :, 2 * SB_W:3 * SB_W].reshape(db, s_len, SB_HEADS, SB_DH)))
        wdt = op.shape[-1]
        op2 = op.reshape(b * lp, wdt)
        os2 = os_.reshape(db * s_len, wdt)
        if i + 1 < n_layers:
            xp = post_proj(op2, pp, gate_block, xp, w_out_b)
            xs = post_proj(os2, ps, gate_block, xs, w_out_b)
        else:
            y_sample = post_proj(os2, ps, gate_block, xs, w_out_b, final_g=norm_g_final).reshape(db, s_len, d)
            y_prompt = post_proj(op2, pp, gate_block, xp, w_out_b, batch=b, final_g=norm_g_final,
                                 skip_rows=FRONT_PAD).reshape(b, seq, d)

    k0_p, v0_p, k0_s, v0_s = states[0]
    ret1_p, ret1_s = states[1]
    k2_p, v2_p, k2_s, v2_s = states[2]
    k3_p, v3_p, k3_s, v3_s = states[3]
    return (y_prompt, y_sample, k0_p, v0_p, k0_s, v0_s, ret1_p, ret1_s,
            k2_p, v2_p, k2_s, v2_s, k3_p, v3_p, k3_s, v3_s)
```

```python
import functools
import math

import jax
import jax.numpy as jnp
from jax import lax
from jax.experimental import pallas as pl
from jax.experimental.pallas import tpu as pltpu

F32 = jnp.float32
BF16 = jnp.bfloat16

CHUNK = 64
N_META = 16
NORM_EPS = 1e-6
N_MIXERS = 3
ROPE_BASE = 10000.0

DA_HEADS = 4
DA_DH = 64
DA_DV = 128
DA_W = 512
RET_HEADS = 4
RET_DK = 256
RET_DV = 512
SB_HEADS = 4
SB_DH = 128
SB_W = 512

LANES = 128
BF16_ROWS = 16
FRONT_PAD = 512
DA_KEYS = 1536
SB_KEYS = 768
ATT_SUB = 256
DA_TQ = 256
SB_TQ = 512
SB_CUM = 128
SB_LINEAR = 100.0
DA_VROWS = DA_DV + BF16_ROWS
LOG2E = math.log2(math.e)
RET_BLOCK = 256
ROW_TILE = 512
VMEM_LIMIT = 48 * 1024 * 1024

NEG_INF = float("-inf")
NT_DIMS = (((1,), (1,)), ((), ()))
TN_DIMS = (((0,), (0,)), ((), ()))


def _cparams(sem):
    return pltpu.CompilerParams(dimension_semantics=sem, vmem_limit_bytes=VMEM_LIMIT)


def _silu(g):
    return g / (1.0 + jnp.exp(-g))


def _softplus(z):
    return jnp.maximum(z, 0.0) + jnp.log(1.0 + jnp.exp(-jnp.abs(z)))


def _norm_proj_kernel(x_ref, g_ref, w_ref, o_ref):
    x = x_ref[...]
    ms = jnp.mean(x * x, axis=-1, keepdims=True)
    h = (x * lax.rsqrt(ms + NORM_EPS) * g_ref[...]).astype(BF16)
    o_ref[...] = jnp.dot(h, w_ref[...], preferred_element_type=F32)


def norm_proj(x, g, w_bf16, *, col_tile=2048):
    rows, d = x.shape
    e = w_bf16.shape[1]
    tr = min(ROW_TILE, rows)
    te = min(col_tile, e)
    return pl.pallas_call(
        _norm_proj_kernel,
        out_shape=jax.ShapeDtypeStruct((rows, e), F32),
        grid=(e // te, rows // tr),
        in_specs=[
            pl.BlockSpec((tr, d), lambda j, i: (i, 0)),
            pl.BlockSpec((1, d), lambda j, i: (0, 0)),
            pl.BlockSpec((d, te), lambda j, i: (0, j)),
        ],
        out_specs=pl.BlockSpec((tr, te), lambda j, i: (i, j)),
        compiler_params=_cparams(("parallel", "parallel")),
        name="norm_proj",
    )(x, g.reshape(1, d), w_bf16)


def _norm_proj_kv_kernel(x_ref, g_ref, w_ref, o_ref, kb_ref, vt_ref, *, width, heads, ones_rows):
    x = x_ref[...]
    ms = jnp.mean(x * x, axis=-1, keepdims=True)
    h = (x * lax.rsqrt(ms + NORM_EPS) * g_ref[...]).astype(BF16)
    p = jnp.dot(h, w_ref[...], preferred_element_type=F32)
    o_ref[...] = p
    kb_ref[...] = p[:, width:2 * width].astype(BF16)
    vt = p[:, 2 * width:3 * width].T.astype(BF16)
    if ones_rows:
        dv = width // heads
        for hd in range(heads):
            base = hd * (dv + ones_rows)
            vt_ref[base:base + dv, :] = vt[hd * dv:(hd + 1) * dv, :]
            vt_ref[base + dv:base + dv + ones_rows, :] = jnp.ones((ones_rows, vt.shape[1]), BF16)
    else:
        vt_ref[...] = vt


def norm_proj_kv(x, g, w_bf16, batch, width, heads, ones_rows):
    rows, d = x.shape
    e = w_bf16.shape[1]
    per = rows // batch
    tr = min(ROW_TILE, per)
    n_per = per // tr
    vrows = width + heads * ones_rows
    return pl.pallas_call(
        functools.partial(_norm_proj_kv_kernel, width=width, heads=heads, ones_rows=ones_rows),
        out_shape=(jax.ShapeDtypeStruct((rows, e), F32),
                   jax.ShapeDtypeStruct((rows, width), BF16),
                   jax.ShapeDtypeStruct((batch, vrows, per), BF16)),
        grid=(rows // tr,),
        in_specs=[
            pl.BlockSpec((tr, d), lambda i: (i, 0)),
            pl.BlockSpec((1, d), lambda i: (0, 0)),
            pl.BlockSpec((d, e), lambda i: (0, 0)),
        ],
        out_specs=(pl.BlockSpec((tr, e), lambda i: (i, 0)),
                   pl.BlockSpec((tr, width), lambda i: (i, 0)),
                   pl.BlockSpec((None, vrows, tr), lambda i: (i // n_per, 0, i % n_per))),
        compiler_params=_cparams(("parallel",)),
        name="norm_proj_kv",
    )(x, g.reshape(1, d), w_bf16)


def _post_kernel(o_ref, gate_ref, x_ref, w_ref, out_ref):
    a = (o_ref[...] * _silu(gate_ref[...])).astype(BF16)
    out_ref[...] = x_ref[...] + jnp.dot(a, w_ref[...], preferred_element_type=F32)


def _post_final_kernel(o_ref, gate_ref, x_ref, w_ref, gf_ref, out_ref):
    a = (o_ref[...] * _silu(gate_ref[...])).astype(BF16)
    xn = x_ref[...] + jnp.dot(a, w_ref[...], preferred_element_type=F32)
    ms = jnp.mean(xn * xn, axis=-1, keepdims=True)
    out_ref[...] = xn * lax.rsqrt(ms + NORM_EPS) * gf_ref[...]


def post_proj(o, proj, gate_block, x, w_bf16, *, batch=1, final_g=None, skip_rows=0):
    rows, wdt = o.shape
    d = x.shape[1]
    per = rows // batch
    tr = min(ROW_TILE, per)
    off = skip_rows // tr
    n_in = per // tr
    n = n_in - off
    in_specs = [
        pl.BlockSpec((tr, wdt), lambda bb, i: (bb * n_in + off + i, 0)),
        pl.BlockSpec((tr, wdt), lambda bb, i: (bb * n_in + off + i, gate_block)),
        pl.BlockSpec((tr, d), lambda bb, i: (bb * n_in + off + i, 0)),
        pl.BlockSpec((wdt, d), lambda bb, i: (0, 0)),
    ]
    args = [o, proj, x, w_bf16]
    kern = _post_kernel
    if final_g is not None:
        in_specs.append(pl.BlockSpec((1, d), lambda bb, i: (0, 0)))
        args.append(final_g.reshape(1, d))
        kern = _post_final_kernel
    return pl.pallas_call(
        kern,
        out_shape=jax.ShapeDtypeStruct((batch * n * tr, d), F32),
        grid=(batch, n),
        in_specs=in_specs,
        out_specs=pl.BlockSpec((tr, d), lambda bb, i: (bb * n + i, 0)),
        compiler_params=_cparams(("parallel", "parallel")),
        name="post_proj",
    )(*args)


def _rope_kernel(inv_ref, cos_ref, sin_ref, *, start):
    tr = cos_ref.shape[0]
    row = pl.program_id(0) * tr + lax.broadcasted_iota(jnp.int32, (tr, LANES), 0)
    ang = (row + start).astype(F32) * inv_ref[...]
    cos_ref[...] = jnp.cos(ang)
    sin_ref[...] = jnp.sin(ang)


def rope_tables(n, start):
    half = RET_DK // 2
    inv = (ROPE_BASE ** (-jnp.linspace(0.0, 1.0, half, dtype=F32))).reshape(1, half)
    tr = min(ROW_TILE, n)
    return pl.pallas_call(
        functools.partial(_rope_kernel, start=start),
        out_shape=(jax.ShapeDtypeStruct((n, half), F32), jax.ShapeDtypeStruct((n, half), F32)),
        grid=(n // tr,),
        in_specs=[pl.BlockSpec((1, half), lambda i: (0, 0))],
        out_specs=(pl.BlockSpec((tr, half), lambda i: (i, 0)), pl.BlockSpec((tr, half), lambda i: (i, 0))),
        compiler_params=_cparams(("parallel",)),
        name="rope_tables",
    )(inv)


def _rotary(x, cos, sin):
    half = RET_DK // 2
    x1, x2 = x[:, :half], x[:, half:]
    return jnp.concatenate([x1 * cos - x2 * sin, x1 * sin + x2 * cos], axis=1)


def _da_split_q(q):
    rows = q.shape[0]
    lane = lax.broadcasted_iota(jnp.int32, (rows, LANES), 1)
    out = []
    for h in range(DA_HEADS):
        qh = q[:, h * LANES:(h + 1) * LANES]
        qa = jnp.where(lane < DA_DH, qh, 0.0)
        qb = jnp.where(lane >= DA_DH, qh, 0.0)
        out.append(jnp.concatenate([qa, qb], axis=0).astype(BF16))
    return out


def _da_update(h, qh, kh, vh, vis, m_s, l_s, acc_s):
    s = lax.dot_general(qh, kh, NT_DIMS, preferred_element_type=F32)
    if vis is not None:
        s = jnp.where(vis, s, NEG_INF)
    m_prev = m_s[h]
    m_new = jnp.maximum(m_prev, jnp.max(s, axis=1, keepdims=True))
    m_use = jnp.where(m_new == NEG_INF, 0.0, m_new)
    alpha = jnp.exp(m_prev - m_use)
    e = jnp.exp(s - m_use)
    l_s[h] = alpha * l_s[h] + jnp.sum(e, axis=1, keepdims=True)
    acc_s[h] = alpha * acc_s[h] + jnp.dot(e.astype(BF16), vh, preferred_element_type=F32)
    m_s[h] = m_new


def _da_finish(lam_ref, g_ref, o_ref, l_s, acc_s, rows, lam_init):
    lam = (jnp.exp(jnp.sum(lam_ref[0:1, :] * lam_ref[1:2, :], axis=1, keepdims=True))
           - jnp.exp(jnp.sum(lam_ref[2:3, :] * lam_ref[3:4, :], axis=1, keepdims=True)) + lam_init)
    for h in range(DA_HEADS):
        l = l_s[h]
        o = acc_s[h] / jnp.where(l == 0.0, 1.0, l)
        d = o[:rows] - lam * o[rows:]
        ms = jnp.mean(d * d, axis=1, keepdims=True)
        o_ref[:, h * DA_DV:(h + 1) * DA_DV] = d * lax.rsqrt(ms + NORM_EPS) * g_ref[...] * (1.0 - lam_init)


def _da_prompt_kernel(qi_ref, kj_ref, q_ref, k_ref, vt_ref, lam_ref, g_ref, o_ref,
                      qs, m_s, acc_s, *, tq, lam_init):
    step = pl.program_id(1)
    qi = qi_ref[step]
    kj = kj_ref[step]
    last = (qi * tq + tq - 1) // DA_KEYS
    on_diag = (kj + 1) * DA_KEYS > qi * tq
    n_sub = DA_KEYS // ATT_SUB

    @pl.when(kj == 0)
    def _():
        q = q_ref[...] * (DA_DH ** -0.5 * LOG2E)
        row = lax.broadcasted_iota(jnp.int32, (LANES, tq), 0)
        for h in range(DA_HEADS):
            qt = q[:, h * LANES:(h + 1) * LANES].T
            qa = jnp.where(row < DA_DH, qt, 0.0)
            qb = jnp.where(row >= DA_DH, qt, 0.0)
            qs[h] = jnp.concatenate([qa, qb], axis=1).astype(BF16)
        m_s[...] = jnp.full(m_s.shape, NEG_INF, F32)
        acc_s[...] = jnp.zeros(acc_s.shape, F32)

    first = FRONT_PAD - N_META

    def run(mode):
        subs = range(n_sub)
        vis = {}
        if mode == "pad":
            subs = [sub for sub in subs if (sub + 1) * ATT_SUB > first]
            for sub in subs:
                if sub * ATT_SUB < first:
                    pk = sub * ATT_SUB + lax.broadcasted_iota(jnp.int32, (ATT_SUB, 2 * tq), 0)
                    vis[sub] = pk >= first
        elif mode == "diag":
            pk0 = kj * DA_KEYS + lax.broadcasted_iota(jnp.int32, (ATT_SUB, 2 * tq), 0)
            pq = qi * tq + (lax.broadcasted_iota(jnp.int32, (ATT_SUB, 2 * tq), 1) & (tq - 1))
            for sub in subs:
                pk = pk0 + sub * ATT_SUB
                vis[sub] = (pk >= first) & ((pk >> 6) <= (pq >> 6))
        items = [(sub, h) for sub in subs for h in range(DA_HEADS)]

        def scores(item):
            sub, h = item
            return jnp.dot(k_ref[sub * ATT_SUB:(sub + 1) * ATT_SUB, h * LANES:(h + 1) * LANES], qs[h],
                           preferred_element_type=F32)

        def softmax(item, s):
            sub, h = item
            if sub in vis:
                s = jnp.where(vis[sub], s, NEG_INF)
            m_prev = m_s[h]
            m_new = jnp.maximum(m_prev, jnp.max(s, axis=0, keepdims=True))
            m_use = jnp.where(m_new == NEG_INF, 0.0, m_new)
            m_s[h] = m_new
            return jnp.exp2(m_prev - m_use), jnp.exp2(s - m_use).astype(BF16)

        def accumulate(item, alpha, e):
            sub, h = item
            vt = vt_ref[h * DA_VROWS:(h + 1) * DA_VROWS, sub * ATT_SUB:(sub + 1) * ATT_SUB]
            acc_s[h] = alpha * acc_s[h] + jnp.dot(vt, e, preferred_element_type=F32)

        n = len(items)
        s_live, e_live = {}, {}
        for t in range(n + 2):
            if t < n:
                s_live[t] = scores(items[t])
            if t >= 2:
                accumulate(items[t - 2], *e_live.pop(t - 2))
            if 1 <= t <= n:
                e_live[t - 1] = softmax(items[t - 1], s_live.pop(t - 1))

    off_diag = jnp.logical_not(on_diag)
    pl.when(on_diag)(lambda: run("diag"))
    pl.when((kj == 0) & off_diag)(lambda: run("pad"))
    pl.when((kj != 0) & off_diag)(lambda: run("open"))

    @pl.when(kj == last)
    def _():
        lam = (jnp.exp(jnp.sum(lam_ref[0:1, :] * lam_ref[1:2, :], axis=1, keepdims=True))
               - jnp.exp(jnp.sum(lam_ref[2:3, :] * lam_ref[3:4, :], axis=1, keepdims=True)) + lam_init)
        for h in range(DA_HEADS):
            acc = acc_s[h]
            l = acc[DA_DV:DA_DV + 1, :]
            o = acc[:DA_DV, :] / jnp.where(l == 0.0, 1.0, l)
            d = o[:, :tq] - lam * o[:, tq:]
            ms = jnp.mean(d * d, axis=0, keepdims=True)
            y = d * lax.rsqrt(ms + NORM_EPS) * g_ref[...] * (1.0 - lam_init)
            o_ref[:, h * DA_DV:(h + 1) * DA_DV] = y.T


def _tri_steps(nq, tq, keys, descending):
    qi, kj = [], []
    for i in range(nq):
        last = (i * tq + tq - 1) // keys
        ks = range(last, -1, -1) if descending else range(last + 1)
        for k in ks:
            qi.append(i)
            kj.append(k)
    return jnp.asarray(qi, jnp.int32), jnp.asarray(kj, jnp.int32)


def da_prompt(proj, kb, vt, lam4, subln_g, lam_init):
    b, lp, _ = proj.shape
    tq = DA_TQ
    qi, kj = _tri_steps(lp // tq, tq, DA_KEYS, descending=False)
    kern = functools.partial(_da_prompt_kernel, tq=tq, lam_init=lam_init)
    grid_spec = pltpu.PrefetchScalarGridSpec(
        num_scalar_prefetch=2,
        grid=(b, qi.shape[0]),
        in_specs=[
            pl.BlockSpec((None, tq, DA_W), lambda bb, s, qi, kj: (bb, qi[s], 0)),
            pl.BlockSpec((None, DA_KEYS, DA_W), lambda bb, s, qi, kj: (bb, kj[s], 0)),
            pl.BlockSpec((None, DA_HEADS * DA_VROWS, DA_KEYS), lambda bb, s, qi, kj: (bb, 0, kj[s])),
            pl.BlockSpec((4, DA_DH), lambda bb, s, qi, kj: (0, 0)),
            pl.BlockSpec((DA_DV, 1), lambda bb, s, qi, kj: (0, 0)),
        ],
        out_specs=pl.BlockSpec((None, tq, DA_W), lambda bb, s, qi, kj: (bb, qi[s], 0)),
        scratch_shapes=[
            pltpu.VMEM((DA_HEADS, LANES, 2 * tq), BF16),
            pltpu.VMEM((DA_HEADS, 1, 2 * tq), F32),
            pltpu.VMEM((DA_HEADS, DA_VROWS, 2 * tq), F32),
        ],
    )
    return pl.pallas_call(
        kern,
        out_shape=jax.ShapeDtypeStruct((b, lp, DA_W), F32),
        grid_spec=grid_spec,
        compiler_params=_cparams(("parallel", "arbitrary")),
        name="da_prompt",
    )(qi, kj, proj, kb, vt, lam4, subln_g.reshape(DA_DV, 1))


def _da_sample_kernel(q_ref, kn_ref, vn_ref, kc_ref, vc_ref, lam_ref, g_ref, o_ref,
                      qs, m_s, l_s, acc_s, *, rows, lam_init):
    j = pl.program_id(1)

    @pl.when(j == 0)
    def _():
        for h, qh in enumerate(_da_split_q(q_ref[...] * (DA_DH ** -0.5))):
            qs[h] = qh
        m_s[...] = jnp.full(m_s.shape, NEG_INF, F32)
        l_s[...] = jnp.zeros(l_s.shape, F32)
        acc_s[...] = jnp.zeros(acc_s.shape, F32)

    for h in range(DA_HEADS):
        kh = kc_ref[:, h * LANES:(h + 1) * LANES].astype(BF16)
        vh = vc_ref[:, h * DA_DV:(h + 1) * DA_DV].astype(BF16)
        _da_update(h, qs[h], kh, vh, None, m_s, l_s, acc_s)

    @pl.when(j == pl.num_programs(1) - 1)
    def _():
        for h in range(DA_HEADS):
            kh = kn_ref[:, h * LANES:(h + 1) * LANES].astype(BF16)
            vh = vn_ref[:, h * DA_DV:(h + 1) * DA_DV].astype(BF16)
            _da_update(h, qs[h], kh, vh, None, m_s, l_s, acc_s)
        _da_finish(lam_ref, g_ref, o_ref, l_s, acc_s, rows, lam_init)


def da_sample(proj, cache_k, cache_v, lam4, subln_g, lam_init):
    db, s, _ = proj.shape
    past = cache_k.shape[1]
    tk = min(1024, past)
    kern = functools.partial(_da_sample_kernel, rows=s, lam_init=lam_init)
    return pl.pallas_call(
        kern,
        out_shape=jax.ShapeDtypeStruct((db, s, DA_W), F32),
        grid=(db, past // tk),
        in_specs=[
            pl.BlockSpec((None, s, DA_W), lambda b, j: (b, 0, 0)),
            pl.BlockSpec((None, s, DA_W), lambda b, j: (b, 0, 1)),
            pl.BlockSpec((None, s, DA_W), lambda b, j: (b, 0, 2)),
            pl.BlockSpec((None, tk, DA_W), lambda b, j: (b, j, 0)),
            pl.BlockSpec((None, tk, DA_W), lambda b, j: (b, j, 0)),
            pl.BlockSpec((4, DA_DH), lambda b, j: (0, 0)),
            pl.BlockSpec((1, DA_DV), lambda b, j: (0, 0)),
        ],
        out_specs=pl.BlockSpec((None, s, DA_W), lambda b, j: (b, 0, 0)),
        scratch_shapes=[
            pltpu.VMEM((DA_HEADS, 2 * s, LANES), BF16),
            pltpu.VMEM((DA_HEADS, 2 * s, 1), F32),
            pltpu.VMEM((DA_HEADS, 2 * s, 1), F32),
            pltpu.VMEM((DA_HEADS, 2 * s, DA_DV), F32),
        ],
        compiler_params=_cparams(("parallel", "arbitrary")),
        name="da_sample",
    )(proj, proj, proj, cache_k, cache_v, lam4, subln_g.reshape(1, DA_DV))


def _sb_update(h, qh, kh, vh, u, vis, r_s, acc_s):
    z = lax.dot_general(qh, kh, NT_DIMS, preferred_element_type=F32)
    lk = -_softplus(z)
    if vis is not None:
        lk = jnp.where(vis, lk, 0.0)
    hi = lk.astype(BF16)
    lo = (lk - hi.astype(F32)).astype(BF16)
    tail = (jnp.dot(hi, u, preferred_element_type=F32) + jnp.dot(lo, u, preferred_element_type=F32)) + r_s[h]
    a = jnp.exp(z + tail)
    if vis is not None:
        a = jnp.where(vis, a, 0.0)
    acc_s[h] = acc_s[h] + jnp.dot(a.astype(BF16), vh, preferred_element_type=F32)
    r_s[h] = r_s[h] + jnp.sum(lk, axis=1, keepdims=True)


def _sb_prompt_kernel(qi_ref, kj_ref, q_ref, k_ref, vt_ref, w_ref, o_ref, qs, r_s, acc_s, *, tq):
    step = pl.program_id(1)
    qi = qi_ref[step]
    kj = kj_ref[step]
    last = (qi * tq + tq - 1) // SB_KEYS
    on_diag = (kj + 1) * SB_KEYS > qi * tq
    n_sub = SB_KEYS // ATT_SUB

    @pl.when(kj == last)
    def _():
        q = q_ref[...] * (SB_DH ** -0.5 * LOG2E)
        for h in range(SB_HEADS):
            qs[h] = q[:, h * SB_DH:(h + 1) * SB_DH].T.astype(BF16)
        r_s[...] = jnp.zeros(r_s.shape, F32)
        acc_s[...] = jnp.zeros(acc_s.shape, F32)

    def run(masked):
        items = [(sub, h) for sub in range(n_sub - 1, -1, -1) for h in range(SB_HEADS)]
        if masked:
            pk0 = kj * SB_KEYS + lax.broadcasted_iota(jnp.int32, (ATT_SUB, tq), 0)
            pq = qi * tq + lax.broadcasted_iota(jnp.int32, (ATT_SUB, tq), 1)
            vis = [pk0 + sub * ATT_SUB < pq for sub in range(n_sub)]
        w2 = w_ref[...]
        n_blk = ATT_SUB // SB_CUM

        def logits(item):
            sub, h = item
            return jnp.dot(k_ref[sub * ATT_SUB:(sub + 1) * ATT_SUB, h * SB_DH:(h + 1) * SB_DH], qs[h],
                           preferred_element_type=F32)

        def split(item, z):
            sub, _ = item
            sp = jnp.where(z > SB_LINEAR, z, jnp.log(1.0 + jnp.exp2(z)) * LOG2E)
            if masked:
                sp = jnp.where(vis[sub], sp, 0.0)
            hi = lax.bitcast_convert_type(lax.bitcast_convert_type(sp, jnp.uint32) & jnp.uint32(0xFFFF0000), F32)
            return hi.astype(BF16), (sp - hi).astype(BF16)

        def suffix(hi, lo):
            return [jnp.dot(w2, jnp.concatenate([hi[b * SB_CUM:(b + 1) * SB_CUM], lo[b * SB_CUM:(b + 1) * SB_CUM]],
                                                axis=0), preferred_element_type=F32) for b in range(n_blk)]

        def weights(item, z, tins):
            sub, h = item
            r = r_s[h]
            tails = [None] * n_blk
            for b in range(n_blk - 1, -1, -1):
                tails[b] = tins[b] + r
                r = r + tins[b][0:1, :]
            r_s[h] = r
            a = jnp.exp2(z - jnp.concatenate(tails, axis=0))
            if masked:
                a = jnp.where(vis[sub], a, 0.0)
            return a.astype(BF16)

        def accumulate(item, a):
            sub, h = item
            vt = vt_ref[h * SB_DH:(h + 1) * SB_DH, sub * ATT_SUB:(sub + 1) * ATT_SUB]
            acc_s[h] = acc_s[h] + jnp.dot(vt, a, preferred_element_type=F32)

        n = len(items)
        z_live, hl_live, t_live, a_live = {}, {}, {}, {}
        for t in range(n + 4):
            if t < n:
                z_live[t] = logits(items[t])
            if 2 <= t < n + 2:
                t_live[t - 2] = suffix(*hl_live.pop(t - 2))
            if t >= 4:
                accumulate(items[t - 4], a_live.pop(t - 4))
            if 1 <= t < n + 1:
                hl_live[t - 1] = split(items[t - 1], z_live[t - 1])
            if 3 <= t < n + 3:
                a_live[t - 3] = weights(items[t - 3], z_live.pop(t - 3), t_live.pop(t - 3))

    pl.when(on_diag)(lambda: run(True))
    pl.when(jnp.logical_not(on_diag))(lambda: run(False))

    @pl.when(kj == 0)
    def _():
        for h in range(SB_HEADS):
            o_ref[:, h * SB_DH:(h + 1) * SB_DH] = acc_s[h].T


def _tri_ge(n, transpose=False):
    idx = jnp.arange(n)
    m = idx[:, None] >= idx[None, :]
    return (m.T if transpose else m).astype(BF16)


def sb_prompt(proj, kb, vt):
    b, lp, _ = proj.shape
    tq = SB_TQ
    qi, kj = _tri_steps(lp // tq, tq, SB_KEYS, descending=True)
    kern = functools.partial(_sb_prompt_kernel, tq=tq)
    grid_spec = pltpu.PrefetchScalarGridSpec(
        num_scalar_prefetch=2,
        grid=(b, qi.shape[0]),
        in_specs=[
            pl.BlockSpec((None, tq, SB_W), lambda bb, s, qi, kj: (bb, qi[s], 0)),
            pl.BlockSpec((None, SB_KEYS, SB_W), lambda bb, s, qi, kj: (bb, kj[s], 0)),
            pl.BlockSpec((None, SB_W, SB_KEYS), lambda bb, s, qi, kj: (bb, 0, kj[s])),
            pl.BlockSpec((SB_CUM, 2 * SB_CUM), lambda bb, s, qi, kj: (0, 0)),
        ],
        out_specs=pl.BlockSpec((None, tq, SB_W), lambda bb, s, qi, kj: (bb, qi[s], 0)),
        scratch_shapes=[
            pltpu.VMEM((SB_HEADS, SB_DH, tq), BF16),
            pltpu.VMEM((SB_HEADS, 1, tq), F32),
            pltpu.VMEM((SB_HEADS, SB_DH, tq), F32),
        ],
    )
    return pl.pallas_call(
        kern,
        out_shape=jax.ShapeDtypeStruct((b, lp, SB_W), F32),
        grid_spec=grid_spec,
        compiler_params=_cparams(("parallel", "arbitrary")),
        name="sb_prompt",
    )(qi, kj, proj, kb, vt, jnp.tile(_tri_ge(SB_CUM, transpose=True), (1, 2)))


def _sb_sample_kernel(q_ref, kn_ref, vn_ref, kc_ref, vc_ref, un_ref, uc_ref, o_ref, qs, r_s, acc_s, *, rows):
    j = pl.program_id(1)

    @pl.when(j == 0)
    def _():
        q = q_ref[...] * (SB_DH ** -0.5)
        for h in range(SB_HEADS):
            qs[h] = q[:, h * SB_DH:(h + 1) * SB_DH].astype(BF16)
        r_s[...] = jnp.zeros(r_s.shape, F32)
        acc_s[...] = jnp.zeros(acc_s.shape, F32)
        iq = lax.broadcasted_iota(jnp.int32, (rows, rows), 0)
        ik = lax.broadcasted_iota(jnp.int32, (rows, rows), 1)
        vis = ik < iq
        un = un_ref[...]
        for h in range(SB_HEADS):
            kh = kn_ref[:, h * SB_DH:(h + 1) * SB_DH].astype(BF16)
            vh = vn_ref[:, h * SB_DH:(h + 1) * SB_DH].astype(BF16)
            _sb_update(h, qs[h], kh, vh, un, vis, r_s, acc_s)

    uc = uc_ref[...]
    for h in range(SB_HEADS):
        kh = kc_ref[:, h * SB_DH:(h + 1) * SB_DH].astype(BF16)
        vh = vc_ref[:, h * SB_DH:(h + 1) * SB_DH].astype(BF16)
        _sb_update(h, qs[h], kh, vh, uc, None, r_s, acc_s)

    @pl.when(j == pl.num_programs(1) - 1)
    def _():
        for h in range(SB_HEADS):
            o_ref[:, h * SB_DH:(h + 1) * SB_DH] = acc_s[h]


def sb_sample(proj, cache_k, cache_v):
    db, s, _ = proj.shape
    past = cache_k.shape[1]
    tk = min(256, past)
    n = past // tk
    kern = functools.partial(_sb_sample_kernel, rows=s)
    return pl.pallas_call(
        kern,
        out_shape=jax.ShapeDtypeStruct((db, s, SB_W), F32),
        grid=(db, n),
        in_specs=[
            pl.BlockSpec((None, s, SB_W), lambda b, j: (b, 0, 0)),
            pl.BlockSpec((None, s, SB_W), lambda b, j: (b, 0, 1)),
            pl.BlockSpec((None, s, SB_W), lambda b, j: (b, 0, 2)),
            pl.BlockSpec((None, tk, SB_W), lambda b, j: (b, n - 1 - j, 0)),
            pl.BlockSpec((None, tk, SB_W), lambda b, j: (b, n - 1 - j, 0)),
            pl.BlockSpec((s, s), lambda b, j: (0, 0)),
            pl.BlockSpec((tk, tk), lambda b, j: (0, 0)),
        ],
        out_specs=pl.BlockSpec((None, s, SB_W), lambda b, j: (b, 0, 0)),
        scratch_shapes=[
            pltpu.VMEM((SB_HEADS, s, SB_DH), BF16),
            pltpu.VMEM((SB_HEADS, s, 1), F32),
            pltpu.VMEM((SB_HEADS, s, SB_DH), F32),
        ],
        compiler_params=_cparams(("parallel", "arbitrary")),
        name="sb_sample",
    )(proj, proj, proj, cache_k, cache_v, _tri_ge(s), _tri_ge(tk))


def _ret_block(q, k, v, cos, sin, state, log_gamma):
    c = q.shape[0]
    qr = _rotary(q, cos, sin)
    kr = _rotary(k, cos, sin) * (RET_DK ** -0.5)
    ii = lax.broadcasted_iota(jnp.int32, (c, c), 0)
    jj = lax.broadcasted_iota(jnp.int32, (c, c), 1)
    rel = (ii - jj).astype(F32)
    decay = jnp.where(rel >= 0.0, jnp.exp(jnp.maximum(rel, 0.0) * log_gamma), 0.0)
    idx = lax.broadcasted_iota(jnp.int32, (c, 1), 0).astype(F32)
    q_dec = (qr * jnp.exp((idx + 1.0) * log_gamma)).astype(BF16)
    k_dec = (kr * jnp.exp((c - 1.0 - idx) * log_gamma)).astype(BF16)
    vb = v.astype(BF16)
    scores = lax.dot_general(qr.astype(BF16), kr.astype(BF16), NT_DIMS, preferred_element_type=F32) * decay
    o = (jnp.dot(scores.astype(BF16), vb, preferred_element_type=F32)
         + jnp.dot(q_dec, state.astype(BF16), preferred_element_type=F32))
    new_state = (jnp.exp(c * log_gamma) * state
                 + lax.dot_general(k_dec, vb, TN_DIMS, preferred_element_type=F32))
    return o, new_state


def _ret_norm(o, g):
    ms = jnp.mean(o * o, axis=1, keepdims=True)
    return o * lax.rsqrt(ms + NORM_EPS) * g


def _log_gamma(h):
    hv = jnp.zeros((1, 1), F32) + (float(h) if isinstance(h, int) else h.astype(F32))
    return jnp.log(1.0 - jnp.exp2(-5.0 - hv))


def _ret_prompt_kernel(q_ref, k_ref, v_ref, cos_ref, sin_ref, g_ref, o_ref, sf_ref, st):
    j = pl.program_id(1)
    c = q_ref.shape[0]

    @pl.when(j == 0)
    def _():
        st[...] = jnp.zeros(st.shape, F32)

    cos, sin = cos_ref[...], sin_ref[...]
    ii = lax.broadcasted_iota(jnp.int32, (c, c), 0)
    jj = lax.broadcasted_iota(jnp.int32, (c, c), 1)
    rel = jnp.maximum(ii - jj, 0).astype(F32)
    idx = lax.broadcasted_iota(jnp.int32, (c, 1), 0).astype(F32)
    heads = range(RET_HEADS)
    lg = [_log_gamma(h) for h in heads]
    qr = [_rotary(q_ref[:, h * RET_DK:(h + 1) * RET_DK], cos, sin) for h in heads]
    kr = [_rotary(k_ref[:, h * RET_DK:(h + 1) * RET_DK], cos, sin) * (RET_DK ** -0.5) for h in heads]
    vb = [v_ref[:, h * RET_DV:(h + 1) * RET_DV].astype(BF16) for h in heads]
    scores = [lax.dot_general(qr[h].astype(BF16), kr[h].astype(BF16), NT_DIMS, preferred_element_type=F32)
              for h in heads]
    cross = [jnp.dot((qr[h] * jnp.exp((idx + 1.0) * lg[h])).astype(BF16), st[h].astype(BF16),
                     preferred_element_type=F32) for h in heads]
    grown = [lax.dot_general((kr[h] * jnp.exp((c - 1.0 - idx) * lg[h])).astype(BF16), vb[h], TN_DIMS,
                             preferred_element_type=F32) for h in heads]
    for h in heads:
        decay = jnp.where(ii >= jj, jnp.exp(rel * lg[h]), 0.0)
        o = jnp.dot((scores[h] * decay).astype(BF16), vb[h], preferred_element_type=F32) + cross[h]
        o_ref[:, h * RET_DV:(h + 1) * RET_DV] = _ret_norm(o, g_ref[h])
        st[h] = jnp.exp(c * lg[h]) * st[h] + grown[h]

    @pl.when(j == pl.num_programs(1) - 1)
    def _():
        sf_ref[...] = st[...]


def ret_prompt(proj, cos, sin, gn_g):
    b, lp, _ = proj.shape
    c = RET_BLOCK
    qk_w = RET_HEADS * RET_DK
    v_w = RET_HEADS * RET_DV
    return pl.pallas_call(
        _ret_prompt_kernel,
        out_shape=(jax.ShapeDtypeStruct((b, lp, v_w), F32),
                   jax.ShapeDtypeStruct((b, RET_HEADS, RET_DK, RET_DV), F32)),
        grid=(b, lp // c),
        in_specs=[
            pl.BlockSpec((None, c, qk_w), lambda bb, j: (bb, j, 0)),
            pl.BlockSpec((None, c, qk_w), lambda bb, j: (bb, j, 1)),
            pl.BlockSpec((None, c, v_w), lambda bb, j: (bb, j, 1)),
            pl.BlockSpec((c, RET_DK // 2), lambda bb, j: (j, 0)),
            pl.BlockSpec((c, RET_DK // 2), lambda bb, j: (j, 0)),
            pl.BlockSpec((RET_HEADS, 1, RET_DV), lambda bb, j: (0, 0, 0)),
        ],
        out_specs=(pl.BlockSpec((None, c, v_w), lambda bb, j: (bb, j, 0)),
                   pl.BlockSpec((None, RET_HEADS, RET_DK, RET_DV), lambda bb, j: (bb, 0, 0, 0))),
        scratch_shapes=[pltpu.VMEM((RET_HEADS, RET_DK, RET_DV), F32)],
        compiler_params=_cparams(("parallel", "arbitrary")),
        name="ret_prompt",
    )(proj, proj, proj, cos, sin, gn_g.reshape(RET_HEADS, 1, RET_DV))


def _ret_sample_kernel(q_ref, k_ref, v_ref, cos_ref, sin_ref, g_ref, s_ref, o_ref, sf_ref):
    o, new_state = _ret_block(q_ref[...], k_ref[...], v_ref[...], cos_ref[...], sin_ref[...], s_ref[...],
                              _log_gamma(pl.program_id(1)))
    o_ref[...] = _ret_norm(o, g_ref[...])
    sf_ref[...] = new_state


def ret_sample(proj, cos, sin, gn_g, state):
    db, s, _ = proj.shape
    kv_off = (RET_HEADS * RET_DK) // RET_DK
    v_off = (2 * RET_HEADS * RET_DK) // RET_DV
    return pl.pallas_call(
        _ret_sample_kernel,
        out_shape=(jax.ShapeDtypeStruct((db, s, RET_HEADS * RET_DV), F32),
                   jax.ShapeDtypeStruct(state.shape, F32)),
        grid=(db, RET_HEADS),
        in_specs=[
            pl.BlockSpec((None, s, RET_DK), lambda b, h: (b, 0, h)),
            pl.BlockSpec((None, s, RET_DK), lambda b, h: (b, 0, kv_off + h)),
            pl.BlockSpec((None, s, RET_DV), lambda b, h: (b, 0, v_off + h)),
            pl.BlockSpec((s, RET_DK // 2), lambda b, h: (0, 0)),
            pl.BlockSpec((s, RET_DK // 2), lambda b, h: (0, 0)),
            pl.BlockSpec((None, 1, RET_DV), lambda b, h: (h, 0, 0)),
            pl.BlockSpec((None, None, RET_DK, RET_DV), lambda b, h: (b, h, 0, 0)),
        ],
        out_specs=(pl.BlockSpec((None, s, RET_DV), lambda b, h: (b, 0, h)),
                   pl.BlockSpec((None, None, RET_DK, RET_DV), lambda b, h: (b, h, 0, 0))),
        compiler_params=_cparams(("parallel", "parallel")),
        name="ret_sample",
    )(proj, proj, proj, cos, sin, gn_g.reshape(RET_HEADS, 1, RET_DV), state)


def kernel(x_prompt, x_sample, cache_k_l0, cache_v_l0, state_ret_l1, cache_k_l2, cache_v_l2, cache_k_l3, cache_v_l3, meta_tokens, norm_g_0, w_in_0, lam_q1_0, lam_k1_0, lam_q2_0, lam_k2_0, subln_g_0, w_out_0, norm_g_1, w_in_1, gn_g_1, w_out_1, norm_g_2, w_in_2, w_out_2, norm_g_3, w_in_3, lam_q1_3, lam_k1_3, lam_q2_3, lam_k2_3, subln_g_3, w_out_3, norm_g_final):
    b, seq, d = x_prompt.shape
    db, s_len, _ = x_sample.shape
    past = cache_k_l0.shape[1]
    assert seq % FRONT_PAD == 0 and N_META <= FRONT_PAD
    lp = FRONT_PAD + seq
    n_real = N_META + seq
    first = FRONT_PAD - N_META

    meta = jnp.broadcast_to(meta_tokens.astype(F32)[None], (b, N_META, d))
    xp = jnp.concatenate([jnp.zeros((b, first, d), F32), meta, x_prompt], axis=1).reshape(b * lp, d)
    xs = x_sample.reshape(db * s_len, d)

    layers = [
        (norm_g_0, w_in_0, w_out_0, (lam_q1_0, lam_k1_0, lam_q2_0, lam_k2_0, subln_g_0), (cache_k_l0, cache_v_l0)),
        (norm_g_1, w_in_1, w_out_1, (gn_g_1,), (state_ret_l1,)),
        (norm_g_2, w_in_2, w_out_2, (), (cache_k_l2, cache_v_l2)),
        (norm_g_3, w_in_3, w_out_3, (lam_q1_3, lam_k1_3, lam_q2_3, lam_k2_3, subln_g_3), (cache_k_l3, cache_v_l3)),
    ]
    states = []
    y_prompt = y_sample = None
    n_layers = len(layers)
    for i, (norm_g, w_in, w_out, extra, cache) in enumerate(layers):
        kind = i % N_MIXERS
        w_in_b = w_in.astype(BF16)
        w_out_b = w_out.astype(BF16)
        if kind == 1:
            pp = norm_proj(xp, norm_g, w_in_b)
        else:
            if kind == 0:
                pp, kb, vt = norm_proj_kv(xp, norm_g, w_in_b, b, DA_W, DA_HEADS, BF16_ROWS)
            else:
                pp, kb, vt = norm_proj_kv(xp, norm_g, w_in_b, b, SB_W, SB_HEADS, 0)
            kb = kb.reshape(b, lp, kb.shape[-1])
        ps = norm_proj(xs, norm_g, w_in_b)
        e = w_in.shape[1]
        pp3 = pp.reshape(b, lp, e)
        ps3 = ps.reshape(db, s_len, e)
        if kind == 0:
            lam_init = 0.8 - 0.6 * math.exp(-0.3 * i)
            lam4 = jnp.stack([v.astype(F32) for v in extra[:4]])
            op = da_prompt(pp3, kb, vt, lam4, extra[4], lam_init)
            os_ = da_sample(ps3, cache[0].reshape(db, past, DA_W), cache[1].reshape(db, past, DA_W),
                            lam4, extra[4], lam_init)
            gate_block = 3
            states.append((
                pp3[:, first:, DA_W:2 * DA_W].reshape(b, n_real, 2 * DA_HEADS, DA_DH),
                pp3[:, first:, 2 * DA_W:3 * DA_W].reshape(b, n_real, DA_HEADS, DA_DV),
                ps3[:, :, DA_W:2 * DA_W].reshape(db, s_len, 2 * DA_HEADS, DA_DH),
                ps3[:, :, 2 * DA_W:3 * DA_W].reshape(db, s_len, DA_HEADS, DA_DV)))
        elif kind == 1:
            cos_p, sin_p = rope_tables(lp, -FRONT_PAD)
            cos_s, sin_s = rope_tables(s_len, past)
            op, sp = ret_prompt(pp3, cos_p, sin_p, extra[0])
            os_, ss = ret_sample(ps3, cos_s, sin_s, extra[0], cache[0])
            gate_block = 2
            states.append((sp, ss))
        else:
            op = sb_prompt(pp3, kb, vt)
            os_ = sb_sample(ps3, cache[0].reshape(db, past, SB_W), cache[1].reshape(db, past, SB_W))
            gate_block = 3
            states.append((
                pp3[:, first:, SB_W:2 * SB_W].reshape(b, n_real, SB_HEADS, SB_DH),
                pp3[:, first:, 2 * SB_W:3 * SB_W].reshape(b, n_real, SB_HEADS, SB_DH),
                ps3[:, :, SB_W:2 * SB_W].reshape(db, s_len, SB_HEADS, SB_DH),
                ps3[:, :, 2 * SB_W:3 * SB_W].reshape(db, s_len, SB_HEADS, SB_DH)))
        wdt = op.shape[-1]
        op2 = op.reshape(b * lp, wdt)
        os2 = os_.reshape(db * s_len, wdt)
        if i + 1 < n_layers:
            xp = post_proj(op2, pp, gate_block, xp, w_out_b)
            xs = post_proj(os2, ps, gate_block, xs, w_out_b)
        else:
            y_sample = post_proj(os2, ps, gate_block, xs, w_out_b, final_g=norm_g_final).reshape(db, s_len, d)
            y_prompt = post_proj(op2, pp, gate_block, xp, w_out_b, batch=b, final_g=norm_g_final,
                                 skip_rows=FRONT_PAD).reshape(b, seq, d)

    k0_p, v0_p, k0_s, v0_s = states[0]
    ret1_p, ret1_s = states[1]
    k2_p, v2_p, k2_s, v2_s = states[2]
    k3_p, v3_p, k3_s, v3_s = states[3]
    return (y_prompt, y_sample, k0_p, v0_p, k0_s, v0_s, ret1_p, ret1_s,
            k2_p, v2_p, k2_s, v2_s, k3_p, v3_p, k3_s, v3_s)
```

```python
import functools
import math

import jax
import jax.numpy as jnp
from jax import lax
from jax.experimental import pallas as pl
from jax.experimental.pallas import tpu as pltpu

F32 = jnp.float32
BF16 = jnp.bfloat16

CHUNK = 64
N_META = 16
NORM_EPS = 1e-6
N_MIXERS = 3
ROPE_BASE = 10000.0

DA_HEADS = 4
DA_DH = 64
DA_DV = 128
DA_W = 512
RET_HEADS = 4
RET_DK = 256
RET_DV = 512
SB_HEADS = 4
SB_DH = 128
SB_W = 512

LANES = 128
BF16_ROWS = 16
FRONT_PAD = 512
DA_KEYS = 768
SB_KEYS = 768
ATT_SUB = 256
DA_TQ = 512
DA_GROUP = 256
DA_LEAD = 1
SB_TQ = 512
SB_CUM = 128
SB_LINEAR = 100.0
DA_VROWS = DA_DV + BF16_ROWS
LOG2E = math.log2(math.e)
RET_BLOCK = 256
ROW_TILE = 512
VMEM_LIMIT = 48 * 1024 * 1024

NEG_INF = float("-inf")
NT_DIMS = (((1,), (1,)), ((), ()))
TN_DIMS = (((0,), (0,)), ((), ()))


def _cparams(sem):
    return pltpu.CompilerParams(dimension_semantics=sem, vmem_limit_bytes=VMEM_LIMIT)


def _silu(g):
    return g / (1.0 + jnp.exp(-g))


def _softplus(z):
    return jnp.maximum(z, 0.0) + jnp.log(1.0 + jnp.exp(-jnp.abs(z)))


def _norm_proj_kernel(x_ref, g_ref, w_ref, o_ref):
    x = x_ref[...]
    ms = jnp.mean(x * x, axis=-1, keepdims=True)
    h = (x * lax.rsqrt(ms + NORM_EPS) * g_ref[...]).astype(BF16)
    o_ref[...] = jnp.dot(h, w_ref[...], preferred_element_type=F32)


def norm_proj(x, g, w_bf16, *, col_tile=2048):
    rows, d = x.shape
    e = w_bf16.shape[1]
    tr = next(t for t in (2 * ROW_TILE, ROW_TILE, rows) if rows % t == 0)
    te = min(col_tile, e)
    return pl.pallas_call(
        _norm_proj_kernel,
        out_shape=jax.ShapeDtypeStruct((rows, e), F32),
        grid=(e // te, rows // tr),
        in_specs=[
            pl.BlockSpec((tr, d), lambda j, i: (i, 0)),
            pl.BlockSpec((1, d), lambda j, i: (0, 0)),
            pl.BlockSpec((d, te), lambda j, i: (0, j)),
        ],
        out_specs=pl.BlockSpec((tr, te), lambda j, i: (i, j)),
        compiler_params=_cparams(("parallel", "parallel")),
        name="norm_proj",
    )(x, g.reshape(1, d), w_bf16)


def _norm_proj_kv_kernel(x_ref, g_ref, w_ref, o_ref, kb_ref, vt_ref, *, width, heads, ones_rows):
    x = x_ref[...]
    ms = jnp.mean(x * x, axis=-1, keepdims=True)
    h = (x * lax.rsqrt(ms + NORM_EPS) * g_ref[...]).astype(BF16)
    p = jnp.dot(h, w_ref[...], preferred_element_type=F32)
    o_ref[...] = p
    kb_ref[...] = p[:, width:2 * width].astype(BF16)
    vt = p[:, 2 * width:3 * width].T.astype(BF16)
    if ones_rows:
        dv = width // heads
        for hd in range(heads):
            base = hd * (dv + ones_rows)
            vt_ref[base:base + dv, :] = vt[hd * dv:(hd + 1) * dv, :]
            vt_ref[base + dv:base + dv + ones_rows, :] = jnp.ones((ones_rows, vt.shape[1]), BF16)
    else:
        vt_ref[...] = vt


def norm_proj_kv(x, g, w_bf16, batch, width, heads, ones_rows):
    rows, d = x.shape
    e = w_bf16.shape[1]
    per = rows // batch
    tr = min(ROW_TILE, per)
    n_per = per // tr
    vrows = width + heads * ones_rows
    return pl.pallas_call(
        functools.partial(_norm_proj_kv_kernel, width=width, heads=heads, ones_rows=ones_rows),
        out_shape=(jax.ShapeDtypeStruct((rows, e), F32),
                   jax.ShapeDtypeStruct((rows, width), BF16),
                   jax.ShapeDtypeStruct((batch, vrows, per), BF16)),
        grid=(rows // tr,),
        in_specs=[
            pl.BlockSpec((tr, d), lambda i: (i, 0)),
            pl.BlockSpec((1, d), lambda i: (0, 0)),
            pl.BlockSpec((d, e), lambda i: (0, 0)),
        ],
        out_specs=(pl.BlockSpec((tr, e), lambda i: (i, 0)),
                   pl.BlockSpec((tr, width), lambda i: (i, 0)),
                   pl.BlockSpec((None, vrows, tr), lambda i: (i // n_per, 0, i % n_per))),
        compiler_params=_cparams(("parallel",)),
        name="norm_proj_kv",
    )(x, g.reshape(1, d), w_bf16)


def _post_kernel(o_ref, gate_ref, x_ref, w_ref, out_ref):
    a = (o_ref[...] * _silu(gate_ref[...])).astype(BF16)
    out_ref[...] = x_ref[...] + jnp.dot(a, w_ref[...], preferred_element_type=F32)


def _post_final_kernel(o_ref, gate_ref, x_ref, w_ref, gf_ref, out_ref):
    a = (o_ref[...] * _silu(gate_ref[...])).astype(BF16)
    xn = x_ref[...] + jnp.dot(a, w_ref[...], preferred_element_type=F32)
    ms = jnp.mean(xn * xn, axis=-1, keepdims=True)
    out_ref[...] = xn * lax.rsqrt(ms + NORM_EPS) * gf_ref[...]


def post_proj(o, proj, gate_block, x, w_bf16, *, batch=1, final_g=None, skip_rows=0):
    rows, wdt = o.shape
    d = x.shape[1]
    per = rows // batch
    tr = min(ROW_TILE, per)
    off = skip_rows // tr
    n_in = per // tr
    n = n_in - off
    in_specs = [
        pl.BlockSpec((tr, wdt), lambda bb, i: (bb * n_in + off + i, 0)),
        pl.BlockSpec((tr, wdt), lambda bb, i: (bb * n_in + off + i, gate_block)),
        pl.BlockSpec((tr, d), lambda bb, i: (bb * n_in + off + i, 0)),
        pl.BlockSpec((wdt, d), lambda bb, i: (0, 0)),
    ]
    args = [o, proj, x, w_bf16]
    kern = _post_kernel
    if final_g is not None:
        in_specs.append(pl.BlockSpec((1, d), lambda bb, i: (0, 0)))
        args.append(final_g.reshape(1, d))
        kern = _post_final_kernel
    return pl.pallas_call(
        kern,
        out_shape=jax.ShapeDtypeStruct((batch * n * tr, d), F32),
        grid=(batch, n),
        in_specs=in_specs,
        out_specs=pl.BlockSpec((tr, d), lambda bb, i: (bb * n + i, 0)),
        compiler_params=_cparams(("parallel", "parallel")),
        name="post_proj",
    )(*args)


def _rope_kernel(inv_ref, cos_ref, sin_ref, *, start):
    tr = cos_ref.shape[0]
    row = pl.program_id(0) * tr + lax.broadcasted_iota(jnp.int32, (tr, LANES), 0)
    ang = (row + start).astype(F32) * inv_ref[...]
    cos_ref[...] = jnp.cos(ang)
    sin_ref[...] = jnp.sin(ang)


def rope_tables(n, start):
    half = RET_DK // 2
    inv = (ROPE_BASE ** (-jnp.linspace(0.0, 1.0, half, dtype=F32))).reshape(1, half)
    tr = min(ROW_TILE, n)
    return pl.pallas_call(
        functools.partial(_rope_kernel, start=start),
        out_shape=(jax.ShapeDtypeStruct((n, half), F32), jax.ShapeDtypeStruct((n, half), F32)),
        grid=(n // tr,),
        in_specs=[pl.BlockSpec((1, half), lambda i: (0, 0))],
        out_specs=(pl.BlockSpec((tr, half), lambda i: (i, 0)), pl.BlockSpec((tr, half), lambda i: (i, 0))),
        compiler_params=_cparams(("parallel",)),
        name="rope_tables",
    )(inv)


def _rotary(x, cos, sin):
    half = RET_DK // 2
    x1, x2 = x[:, :half], x[:, half:]
    return jnp.concatenate([x1 * cos - x2 * sin, x1 * sin + x2 * cos], axis=1)


def _da_split_q(q):
    rows = q.shape[0]
    lane = lax.broadcasted_iota(jnp.int32, (rows, LANES), 1)
    out = []
    for h in range(DA_HEADS):
        qh = q[:, h * LANES:(h + 1) * LANES]
        qa = jnp.where(lane < DA_DH, qh, 0.0)
        qb = jnp.where(lane >= DA_DH, qh, 0.0)
        out.append(jnp.concatenate([qa, qb], axis=0).astype(BF16))
    return out


def _da_update(h, qh, kh, vh, vis, m_s, l_s, acc_s):
    s = lax.dot_general(qh, kh, NT_DIMS, preferred_element_type=F32)
    if vis is not None:
        s = jnp.where(vis, s, NEG_INF)
    m_prev = m_s[h]
    m_new = jnp.maximum(m_prev, jnp.max(s, axis=1, keepdims=True))
    m_use = jnp.where(m_new == NEG_INF, 0.0, m_new)
    alpha = jnp.exp(m_prev - m_use)
    e = jnp.exp(s - m_use)
    l_s[h] = alpha * l_s[h] + jnp.sum(e, axis=1, keepdims=True)
    acc_s[h] = alpha * acc_s[h] + jnp.dot(e.astype(BF16), vh, preferred_element_type=F32)
    m_s[h] = m_new


def _da_finish(lam_ref, g_ref, o_ref, l_s, acc_s, rows, lam_init):
    lam = (jnp.exp(jnp.sum(lam_ref[0:1, :] * lam_ref[1:2, :], axis=1, keepdims=True))
           - jnp.exp(jnp.sum(lam_ref[2:3, :] * lam_ref[3:4, :], axis=1, keepdims=True)) + lam_init)
    for h in range(DA_HEADS):
        l = l_s[h]
        o = acc_s[h] / jnp.where(l == 0.0, 1.0, l)
        d = o[:rows] - lam * o[rows:]
        ms = jnp.mean(d * d, axis=1, keepdims=True)
        o_ref[:, h * DA_DV:(h + 1) * DA_DV] = d * lax.rsqrt(ms + NORM_EPS) * g_ref[...] * (1.0 - lam_init)


def _da_prompt_kernel(qi_ref, kj_ref, q_ref, k_ref, vt_ref, lam_ref, g_ref, o_ref,
                      qs, m_s, acc_s, *, tq, lam_init):
    step = pl.program_id(1)
    qi = qi_ref[step]
    kj = kj_ref[step]
    last = (qi * tq + tq - 1) // DA_KEYS
    on_diag = (kj + 1) * DA_KEYS > qi * tq
    n_sub = DA_KEYS // ATT_SUB
    n_grp = tq // DA_GROUP
    width = 2 * DA_GROUP

    @pl.when(kj == 0)
    def _():
        q = q_ref[...] * (DA_DH ** -0.5 * LOG2E)
        row = lax.broadcasted_iota(jnp.int32, (LANES, DA_GROUP), 0)
        for g in range(n_grp):
            for h in range(DA_HEADS):
                qt = q[g * DA_GROUP:(g + 1) * DA_GROUP, h * LANES:(h + 1) * LANES].T
                qa = jnp.where(row < DA_DH, qt, 0.0)
                qb = jnp.where(row >= DA_DH, qt, 0.0)
                qs[g * DA_HEADS + h] = jnp.concatenate([qa, qb], axis=1).astype(BF16)
        m_s[...] = jnp.full(m_s.shape, NEG_INF, F32)
        acc_s[...] = jnp.zeros(acc_s.shape, F32)

    first = FRONT_PAD - N_META

    def run(mode):
        subs = range(n_sub)
        vis = {}
        if mode == "pad":
            subs = [sub for sub in subs if (sub + 1) * ATT_SUB > first]
            for sub in subs:
                if sub * ATT_SUB < first:
                    pk = sub * ATT_SUB + lax.broadcasted_iota(jnp.int32, (ATT_SUB, width), 0)
                    for g in range(n_grp):
                        vis[sub, g] = pk >= first
        elif mode == "diag":
            pk0 = kj * DA_KEYS + lax.broadcasted_iota(jnp.int32, (ATT_SUB, width), 0)
            pq0 = qi * tq + (lax.broadcasted_iota(jnp.int32, (ATT_SUB, width), 1) & (DA_GROUP - 1))
            for sub in subs:
                pk = pk0 + sub * ATT_SUB
                for g in range(n_grp):
                    vis[sub, g] = (pk >= first) & ((pk >> 6) <= ((pq0 + g * DA_GROUP) >> 6))
        items = [(sub, g, h) for sub in subs for g in range(n_grp) for h in range(DA_HEADS)]

        def scores(item):
            sub, g, h = item
            return jnp.dot(k_ref[sub * ATT_SUB:(sub + 1) * ATT_SUB, h * LANES:(h + 1) * LANES],
                           qs[g * DA_HEADS + h], preferred_element_type=F32)

        def softmax(item, s):
            sub, g, h = item
            if (sub, g) in vis:
                s = jnp.where(vis[sub, g], s, NEG_INF)
            m_prev = m_s[g * DA_HEADS + h]
            m_new = jnp.maximum(m_prev, jnp.max(s, axis=0, keepdims=True))
            m_use = jnp.where(m_new == NEG_INF, 0.0, m_new)
            m_s[g * DA_HEADS + h] = m_new
            return jnp.exp2(m_prev - m_use), jnp.exp2(s - m_use).astype(BF16)

        def accumulate(item, alpha, e):
            sub, g, h = item
            vt = vt_ref[h * DA_VROWS:(h + 1) * DA_VROWS, sub * ATT_SUB:(sub + 1) * ATT_SUB]
            acc_s[g * DA_HEADS + h] = alpha * acc_s[g * DA_HEADS + h] + jnp.dot(vt, e, preferred_element_type=F32)

        n = len(items)
        s_live, e_live = {}, {}
        for t in range(n + DA_LEAD + 1):
            if t < n:
                s_live[t] = scores(items[t])
            if DA_LEAD + 1 <= t:
                accumulate(items[t - DA_LEAD - 1], *e_live.pop(t - DA_LEAD - 1))
            if DA_LEAD <= t < n + DA_LEAD:
                e_live[t - DA_LEAD] = softmax(items[t - DA_LEAD], s_live.pop(t - DA_LEAD))

    off_diag = jnp.logical_not(on_diag)
    pl.when(on_diag)(lambda: run("diag"))
    pl.when((kj == 0) & off_diag)(lambda: run("pad"))
    pl.when((kj != 0) & off_diag)(lambda: run("open"))

    @pl.when(kj == last)
    def _():
        lam = (jnp.exp(jnp.sum(lam_ref[0:1, :] * lam_ref[1:2, :], axis=1, keepdims=True))
               - jnp.exp(jnp.sum(lam_ref[2:3, :] * lam_ref[3:4, :], axis=1, keepdims=True)) + lam_init)
        for g in range(n_grp):
            for h in range(DA_HEADS):
                acc = acc_s[g * DA_HEADS + h]
                l = acc[DA_DV:DA_DV + 1, :]
                o = acc[:DA_DV, :] / jnp.where(l == 0.0, 1.0, l)
                d = o[:, :DA_GROUP] - lam * o[:, DA_GROUP:]
                ms = jnp.mean(d * d, axis=0, keepdims=True)
                y = d * lax.rsqrt(ms + NORM_EPS) * g_ref[...] * (1.0 - lam_init)
                o_ref[g * DA_GROUP:(g + 1) * DA_GROUP, h * DA_DV:(h + 1) * DA_DV] = y.T


def _tri_steps(nq, tq, keys, descending):
    qi, kj = [], []
    for i in range(nq):
        last = (i * tq + tq - 1) // keys
        ks = range(last, -1, -1) if descending else range(last + 1)
        for k in ks:
            qi.append(i)
            kj.append(k)
    return jnp.asarray(qi, jnp.int32), jnp.asarray(kj, jnp.int32)


def da_prompt(proj, kb, vt, lam4, subln_g, lam_init):
    b, lp, _ = proj.shape
    tq = DA_TQ
    n_grp = tq // DA_GROUP
    qi, kj = _tri_steps(lp // tq, tq, DA_KEYS, descending=False)
    kern = functools.partial(_da_prompt_kernel, tq=tq, lam_init=lam_init)
    grid_spec = pltpu.PrefetchScalarGridSpec(
        num_scalar_prefetch=2,
        grid=(b, qi.shape[0]),
        in_specs=[
            pl.BlockSpec((None, tq, DA_W), lambda bb, s, qi, kj: (bb, qi[s], 0)),
            pl.BlockSpec((None, DA_KEYS, DA_W), lambda bb, s, qi, kj: (bb, kj[s], 0)),
            pl.BlockSpec((None, DA_HEADS * DA_VROWS, DA_KEYS), lambda bb, s, qi, kj: (bb, 0, kj[s])),
            pl.BlockSpec((4, DA_DH), lambda bb, s, qi, kj: (0, 0)),
            pl.BlockSpec((DA_DV, 1), lambda bb, s, qi, kj: (0, 0)),
        ],
        out_specs=pl.BlockSpec((None, tq, DA_W), lambda bb, s, qi, kj: (bb, qi[s], 0)),
        scratch_shapes=[
            pltpu.VMEM((n_grp * DA_HEADS, LANES, 2 * DA_GROUP), BF16),
            pltpu.VMEM((n_grp * DA_HEADS, 1, 2 * DA_GROUP), F32),
            pltpu.VMEM((n_grp * DA_HEADS, DA_VROWS, 2 * DA_GROUP), F32),
        ],
    )
    return pl.pallas_call(
        kern,
        out_shape=jax.ShapeDtypeStruct((b, lp, DA_W), F32),
        grid_spec=grid_spec,
        compiler_params=_cparams(("parallel", "arbitrary")),
        name="da_prompt",
    )(qi, kj, proj, kb, vt, lam4, subln_g.reshape(DA_DV, 1))


def _da_sample_kernel(q_ref, kn_ref, vn_ref, kc_ref, vc_ref, lam_ref, g_ref, o_ref,
                      qs, m_s, l_s, acc_s, *, rows, lam_init):
    j = pl.program_id(1)

    @pl.when(j == 0)
    def _():
        for h, qh in enumerate(_da_split_q(q_ref[...] * (DA_DH ** -0.5))):
            qs[h] = qh
        m_s[...] = jnp.full(m_s.shape, NEG_INF, F32)
        l_s[...] = jnp.zeros(l_s.shape, F32)
        acc_s[...] = jnp.zeros(acc_s.shape, F32)

    for h in range(DA_HEADS):
        kh = kc_ref[:, h * LANES:(h + 1) * LANES].astype(BF16)
        vh = vc_ref[:, h * DA_DV:(h + 1) * DA_DV].astype(BF16)
        _da_update(h, qs[h], kh, vh, None, m_s, l_s, acc_s)

    @pl.when(j == pl.num_programs(1) - 1)
    def _():
        for h in range(DA_HEADS):
            kh = kn_ref[:, h * LANES:(h + 1) * LANES].astype(BF16)
            vh = vn_ref[:, h * DA_DV:(h + 1) * DA_DV].astype(BF16)
            _da_update(h, qs[h], kh, vh, None, m_s, l_s, acc_s)
        _da_finish(lam_ref, g_ref, o_ref, l_s, acc_s, rows, lam_init)


def da_sample(proj, cache_k, cache_v, lam4, subln_g, lam_init):
    db, s, _ = proj.shape
    past = cache_k.shape[1]
    tk = min(1024, past)
    kern = functools.partial(_da_sample_kernel, rows=s, lam_init=lam_init)
    return pl.pallas_call(
        kern,
        out_shape=jax.ShapeDtypeStruct((db, s, DA_W), F32),
        grid=(db, past // tk),
        in_specs=[
            pl.BlockSpec((None, s, DA_W), lambda b, j: (b, 0, 0)),
            pl.BlockSpec((None, s, DA_W), lambda b, j: (b, 0, 1)),
            pl.BlockSpec((None, s, DA_W), lambda b, j: (b, 0, 2)),
            pl.BlockSpec((None, tk, DA_W), lambda b, j: (b, j, 0)),
            pl.BlockSpec((None, tk, DA_W), lambda b, j: (b, j, 0)),
            pl.BlockSpec((4, DA_DH), lambda b, j: (0, 0)),
            pl.BlockSpec((1, DA_DV), lambda b, j: (0, 0)),
        ],
        out_specs=pl.BlockSpec((None, s, DA_W), lambda b, j: (b, 0, 0)),
        scratch_shapes=[
            pltpu.VMEM((DA_HEADS, 2 * s, LANES), BF16),
            pltpu.VMEM((DA_HEADS, 2 * s, 1), F32),
            pltpu.VMEM((DA_HEADS, 2 * s, 1), F32),
            pltpu.VMEM((DA_HEADS, 2 * s, DA_DV), F32),
        ],
        compiler_params=_cparams(("parallel", "arbitrary")),
        name="da_sample",
    )(proj, proj, proj, cache_k, cache_v, lam4, subln_g.reshape(1, DA_DV))


def _sb_update(h, qh, kh, vh, u, vis, r_s, acc_s):
    z = lax.dot_general(qh, kh, NT_DIMS, preferred_element_type=F32)
    lk = -_softplus(z)
    if vis is not None:
        lk = jnp.where(vis, lk, 0.0)
    hi = lk.astype(BF16)
    lo = (lk - hi.astype(F32)).astype(BF16)
    tail = (jnp.dot(hi, u, preferred_element_type=F32) + jnp.dot(lo, u, preferred_element_type=F32)) + r_s[h]
    a = jnp.exp(z + tail)
    if vis is not None:
        a = jnp.where(vis, a, 0.0)
    acc_s[h] = acc_s[h] + jnp.dot(a.astype(BF16), vh, preferred_element_type=F32)
    r_s[h] = r_s[h] + jnp.sum(lk, axis=1, keepdims=True)


def _sb_prompt_kernel(qi_ref, kj_ref, q_ref, k_ref, vt_ref, w_ref, o_ref, qs, r_s, acc_s, *, tq):
    step = pl.program_id(1)
    qi = qi_ref[step]
    kj = kj_ref[step]
    last = (qi * tq + tq - 1) // SB_KEYS
    on_diag = (kj + 1) * SB_KEYS > qi * tq
    n_sub = SB_KEYS // ATT_SUB

    @pl.when(kj == last)
    def _():
        q = q_ref[...] * (SB_DH ** -0.5 * LOG2E)
        for h in range(SB_HEADS):
            qs[h] = q[:, h * SB_DH:(h + 1) * SB_DH].T.astype(BF16)
        r_s[...] = jnp.zeros(r_s.shape, F32)
        acc_s[...] = jnp.zeros(acc_s.shape, F32)

    def run(masked):
        items = [(sub, h) for sub in range(n_sub - 1, -1, -1) for h in range(SB_HEADS)]
        if masked:
            pk0 = kj * SB_KEYS + lax.broadcasted_iota(jnp.int32, (ATT_SUB, tq), 0)
            pq = qi * tq + lax.broadcasted_iota(jnp.int32, (ATT_SUB, tq), 1)
            vis = [pk0 + sub * ATT_SUB < pq for sub in range(n_sub)]
        w2 = w_ref[...]
        n_blk = ATT_SUB // SB_CUM

        def logits(item):
            sub, h = item
            return jnp.dot(k_ref[sub * ATT_SUB:(sub + 1) * ATT_SUB, h * SB_DH:(h + 1) * SB_DH], qs[h],
                           preferred_element_type=F32)

        def split(item, z):
            sub, _ = item
            sp = jnp.where(z > SB_LINEAR, z, jnp.log(1.0 + jnp.exp2(z)) * LOG2E)
            if masked:
                sp = jnp.where(vis[sub], sp, 0.0)
            hi = lax.bitcast_convert_type(lax.bitcast_convert_type(sp, jnp.uint32) & jnp.uint32(0xFFFF0000), F32)
            return hi.astype(BF16), (sp - hi).astype(BF16)

        def suffix(hi, lo):
            return [jnp.dot(w2, jnp.concatenate([hi[b * SB_CUM:(b + 1) * SB_CUM], lo[b * SB_CUM:(b + 1) * SB_CUM]],
                                                axis=0), preferred_element_type=F32) for b in range(n_blk)]

        def weights(item, z, tins):
            sub, h = item
            r = r_s[h]
            tails = [None] * n_blk
            for b in range(n_blk - 1, -1, -1):
                tails[b] = tins[b] + r
                r = r + tins[b][0:1, :]
            r_s[h] = r
            a = jnp.exp2(z - jnp.concatenate(tails, axis=0))
            if masked:
                a = jnp.where(vis[sub], a, 0.0)
            return a.astype(BF16)

        def accumulate(item, a):
            sub, h = item
            vt = vt_ref[h * SB_DH:(h + 1) * SB_DH, sub * ATT_SUB:(sub + 1) * ATT_SUB]
            acc_s[h] = acc_s[h] + jnp.dot(vt, a, preferred_element_type=F32)

        n = len(items)
        z_live, hl_live, t_live, a_live = {}, {}, {}, {}
        for t in range(n + 4):
            if t < n:
                z_live[t] = logits(items[t])
            if 2 <= t < n + 2:
                t_live[t - 2] = suffix(*hl_live.pop(t - 2))
            if t >= 4:
                accumulate(items[t - 4], a_live.pop(t - 4))
            if 1 <= t < n + 1:
                hl_live[t - 1] = split(items[t - 1], z_live[t - 1])
            if 3 <= t < n + 3:
                a_live[t - 3] = weights(items[t - 3], z_live.pop(t - 3), t_live.pop(t - 3))

    pl.when(on_diag)(lambda: run(True))
    pl.when(jnp.logical_not(on_diag))(lambda: run(False))

    @pl.when(kj == 0)
    def _():
        for h in range(SB_HEADS):
            o_ref[:, h * SB_DH:(h + 1) * SB_DH] = acc_s[h].T


def _tri_ge(n, transpose=False):
    idx = jnp.arange(n)
    m = idx[:, None] >= idx[None, :]
    return (m.T if transpose else m).astype(BF16)


def sb_prompt(proj, kb, vt):
    b, lp, _ = proj.shape
    tq = SB_TQ
    qi, kj = _tri_steps(lp // tq, tq, SB_KEYS, descending=True)
    kern = functools.partial(_sb_prompt_kernel, tq=tq)
    grid_spec = pltpu.PrefetchScalarGridSpec(
        num_scalar_prefetch=2,
        grid=(b, qi.shape[0]),
        in_specs=[
            pl.BlockSpec((None, tq, SB_W), lambda bb, s, qi, kj: (bb, qi[s], 0)),
            pl.BlockSpec((None, SB_KEYS, SB_W), lambda bb, s, qi, kj: (bb, kj[s], 0)),
            pl.BlockSpec((None, SB_W, SB_KEYS), lambda bb, s, qi, kj: (bb, 0, kj[s])),
            pl.BlockSpec((SB_CUM, 2 * SB_CUM), lambda bb, s, qi, kj: (0, 0)),
        ],
        out_specs=pl.BlockSpec((None, tq, SB_W), lambda bb, s, qi, kj: (bb, qi[s], 0)),
        scratch_shapes=[
            pltpu.VMEM((SB_HEADS, SB_DH, tq), BF16),
            pltpu.VMEM((SB_HEADS, 1, tq), F32),
            pltpu.VMEM((SB_HEADS, SB_DH, tq), F32),
        ],
    )
    return pl.pallas_call(
        kern,
        out_shape=jax.ShapeDtypeStruct((b, lp, SB_W), F32),
        grid_spec=grid_spec,
        compiler_params=_cparams(("parallel", "arbitrary")),
        name="sb_prompt",
    )(qi, kj, proj, kb, vt, jnp.tile(_tri_ge(SB_CUM, transpose=True), (1, 2)))


def _sb_sample_kernel(q_ref, kn_ref, vn_ref, kc_ref, vc_ref, un_ref, uc_ref, o_ref, qs, r_s, acc_s, *, rows):
    j = pl.program_id(1)

    @pl.when(j == 0)
    def _():
        q = q_ref[...] * (SB_DH ** -0.5)
        for h in range(SB_HEADS):
            qs[h] = q[:, h * SB_DH:(h + 1) * SB_DH].astype(BF16)
        r_s[...] = jnp.zeros(r_s.shape, F32)
        acc_s[...] = jnp.zeros(acc_s.shape, F32)
        iq = lax.broadcasted_iota(jnp.int32, (rows, rows), 0)
        ik = lax.broadcasted_iota(jnp.int32, (rows, rows), 1)
        vis = ik < iq
        un = un_ref[...]
        for h in range(SB_HEADS):
            kh = kn_ref[:, h * SB_DH:(h + 1) * SB_DH].astype(BF16)
            vh = vn_ref[:, h * SB_DH:(h + 1) * SB_DH].astype(BF16)
            _sb_update(h, qs[h], kh, vh, un, vis, r_s, acc_s)

    uc = uc_ref[...]
    for h in range(SB_HEADS):
        kh = kc_ref[:, h * SB_DH:(h + 1) * SB_DH].astype(BF16)
        vh = vc_ref[:, h * SB_DH:(h + 1) * SB_DH].astype(BF16)
        _sb_update(h, qs[h], kh, vh, uc, None, r_s, acc_s)

    @pl.when(j == pl.num_programs(1) - 1)
    def _():
        for h in range(SB_HEADS):
            o_ref[:, h * SB_DH:(h + 1) * SB_DH] = acc_s[h]


def sb_sample(proj, cache_k, cache_v):
    db, s, _ = proj.shape
    past = cache_k.shape[1]
    tk = min(512, past)
    n = past // tk
    kern = functools.partial(_sb_sample_kernel, rows=s)
    return pl.pallas_call(
        kern,
        out_shape=jax.ShapeDtypeStruct((db, s, SB_W), F32),
        grid=(db, n),
        in_specs=[
            pl.BlockSpec((None, s, SB_W), lambda b, j: (b, 0, 0)),
            pl.BlockSpec((None, s, SB_W), lambda b, j: (b, 0, 1)),
            pl.BlockSpec((None, s, SB_W), lambda b, j: (b, 0, 2)),
            pl.BlockSpec((None, tk, SB_W), lambda b, j: (b, n - 1 - j, 0)),
            pl.BlockSpec((None, tk, SB_W), lambda b, j: (b, n - 1 - j, 0)),
            pl.BlockSpec((s, s), lambda b, j: (0, 0)),
            pl.BlockSpec((tk, tk), lambda b, j: (0, 0)),
        ],
        out_specs=pl.BlockSpec((None, s, SB_W), lambda b, j: (b, 0, 0)),
        scratch_shapes=[
            pltpu.VMEM((SB_HEADS, s, SB_DH), BF16),
            pltpu.VMEM((SB_HEADS, s, 1), F32),
            pltpu.VMEM((SB_HEADS, s, SB_DH), F32),
        ],
        compiler_params=_cparams(("parallel", "arbitrary")),
        name="sb_sample",
    )(proj, proj, proj, cache_k, cache_v, _tri_ge(s), _tri_ge(tk))


def _ret_block(q, k, v, cos, sin, state, log_gamma):
    c = q.shape[0]
    qr = _rotary(q, cos, sin)
    kr = _rotary(k, cos, sin) * (RET_DK ** -0.5)
    ii = lax.broadcasted_iota(jnp.int32, (c, c), 0)
    jj = lax.broadcasted_iota(jnp.int32, (c, c), 1)
    rel = (ii - jj).astype(F32)
    decay = jnp.where(rel >= 0.0, jnp.exp(jnp.maximum(rel, 0.0) * log_gamma), 0.0)
    idx = lax.broadcasted_iota(jnp.int32, (c, 1), 0).astype(F32)
    q_dec = (qr * jnp.exp((idx + 1.0) * log_gamma)).astype(BF16)
    k_dec = (kr * jnp.exp((c - 1.0 - idx) * log_gamma)).astype(BF16)
    vb = v.astype(BF16)
    scores = lax.dot_general(qr.astype(BF16), kr.astype(BF16), NT_DIMS, preferred_element_type=F32) * decay
    o = (jnp.dot(scores.astype(BF16), vb, preferred_element_type=F32)
         + jnp.dot(q_dec, state.astype(BF16), preferred_element_type=F32))
    new_state = (jnp.exp(c * log_gamma) * state
                 + lax.dot_general(k_dec, vb, TN_DIMS, preferred_element_type=F32))
    return o, new_state


def _ret_norm(o, g):
    ms = jnp.mean(o * o, axis=1, keepdims=True)
    return o * lax.rsqrt(ms + NORM_EPS) * g


def _log_gamma(h):
    hv = jnp.zeros((1, 1), F32) + (float(h) if isinstance(h, int) else h.astype(F32))
    return jnp.log(1.0 - jnp.exp2(-5.0 - hv))


def _ret_prompt_kernel(q_ref, k_ref, v_ref, cos_ref, sin_ref, g_ref, o_ref, sf_ref, st):
    j = pl.program_id(1)
    c = q_ref.shape[0]

    @pl.when(j == 0)
    def _():
        st[...] = jnp.zeros(st.shape, F32)

    cos, sin = cos_ref[...], sin_ref[...]
    ii = lax.broadcasted_iota(jnp.int32, (c, c), 0)
    jj = lax.broadcasted_iota(jnp.int32, (c, c), 1)
    rel = jnp.maximum(ii - jj, 0).astype(F32)
    idx = lax.broadcasted_iota(jnp.int32, (c, 1), 0).astype(F32)
    heads = range(RET_HEADS)
    lg = [_log_gamma(h) for h in heads]
    qr = [_rotary(q_ref[:, h * RET_DK:(h + 1) * RET_DK], cos, sin) for h in heads]
    kr = [_rotary(k_ref[:, h * RET_DK:(h + 1) * RET_DK], cos, sin) * (RET_DK ** -0.5) for h in heads]
    vb = [v_ref[:, h * RET_DV:(h + 1) * RET_DV].astype(BF16) for h in heads]
    scores = [lax.dot_general(qr[h].astype(BF16), kr[h].astype(BF16), NT_DIMS, preferred_element_type=F32)
              for h in heads]
    cross = [jnp.dot((qr[h] * jnp.exp((idx + 1.0) * lg[h])).astype(BF16), st[h].astype(BF16),
                     preferred_element_type=F32) for h in heads]
    grown = [lax.dot_general((kr[h] * jnp.exp((c - 1.0 - idx) * lg[h])).astype(BF16), vb[h], TN_DIMS,
                             preferred_element_type=F32) for h in heads]
    for h in heads:
        decay = jnp.where(ii >= jj, jnp.exp(rel * lg[h]), 0.0)
        o = jnp.dot((scores[h] * decay).astype(BF16), vb[h], preferred_element_type=F32) + cross[h]
        o_ref[:, h * RET_DV:(h + 1) * RET_DV] = _ret_norm(o, g_ref[h])
        st[h] = jnp.exp(c * lg[h]) * st[h] + grown[h]

    @pl.when(j == pl.num_programs(1) - 1)
    def _():
        sf_ref[...] = st[...]


def ret_prompt(proj, cos, sin, gn_g):
    b, lp, _ = proj.shape
    c = RET_BLOCK
    qk_w = RET_HEADS * RET_DK
    v_w = RET_HEADS * RET_DV
    return pl.pallas_call(
        _ret_prompt_kernel,
        out_shape=(jax.ShapeDtypeStruct((b, lp, v_w), F32),
                   jax.ShapeDtypeStruct((b, RET_HEADS, RET_DK, RET_DV), F32)),
        grid=(b, lp // c),
        in_specs=[
            pl.BlockSpec((None, c, qk_w), lambda bb, j: (bb, j, 0)),
            pl.BlockSpec((None, c, qk_w), lambda bb, j: (bb, j, 1)),
            pl.BlockSpec((None, c, v_w), lambda bb, j: (bb, j, 1)),
            pl.BlockSpec((c, RET_DK // 2), lambda bb, j: (j, 0)),
            pl.BlockSpec((c, RET_DK // 2), lambda bb, j: (j, 0)),
            pl.BlockSpec((RET_HEADS, 1, RET_DV), lambda bb, j: (0, 0, 0)),
        ],
        out_specs=(pl.BlockSpec((None, c, v_w), lambda bb, j: (bb, j, 0)),
                   pl.BlockSpec((None, RET_HEADS, RET_DK, RET_DV), lambda bb, j: (bb, 0, 0, 0))),
        scratch_shapes=[pltpu.VMEM((RET_HEADS, RET_DK, RET_DV), F32)],
        compiler_params=_cparams(("parallel", "arbitrary")),
        name="ret_prompt",
    )(proj, proj, proj, cos, sin, gn_g.reshape(RET_HEADS, 1, RET_DV))


def _ret_sample_kernel(q_ref, k_ref, v_ref, cos_ref, sin_ref, g_ref, s_ref, o_ref, sf_ref):
    o, new_state = _ret_block(q_ref[...], k_ref[...], v_ref[...], cos_ref[...], sin_ref[...], s_ref[...],
                              _log_gamma(pl.program_id(1)))
    o_ref[...] = _ret_norm(o, g_ref[...])
    sf_ref[...] = new_state


def ret_sample(proj, cos, sin, gn_g, state):
    db, s, _ = proj.shape
    kv_off = (RET_HEADS * RET_DK) // RET_DK
    v_off = (2 * RET_HEADS * RET_DK) // RET_DV
    return pl.pallas_call(
        _ret_sample_kernel,
        out_shape=(jax.ShapeDtypeStruct((db, s, RET_HEADS * RET_DV), F32),
                   jax.ShapeDtypeStruct(state.shape, F32)),
        grid=(db, RET_HEADS),
        in_specs=[
            pl.BlockSpec((None, s, RET_DK), lambda b, h: (b, 0, h)),
            pl.BlockSpec((None, s, RET_DK), lambda b, h: (b, 0, kv_off + h)),
            pl.BlockSpec((None, s, RET_DV), lambda b, h: (b, 0, v_off + h)),
            pl.BlockSpec((s, RET_DK // 2), lambda b, h: (0, 0)),
            pl.BlockSpec((s, RET_DK // 2), lambda b, h: (0, 0)),
            pl.BlockSpec((None, 1, RET_DV), lambda b, h: (h, 0, 0)),
            pl.BlockSpec((None, None, RET_DK, RET_DV), lambda b, h: (b, h, 0, 0)),
        ],
        out_specs=(pl.BlockSpec((None, s, RET_DV), lambda b, h: (b, 0, h)),
                   pl.BlockSpec((None, None, RET_DK, RET_DV), lambda b, h: (b, h, 0, 0))),
        compiler_params=_cparams(("parallel", "parallel")),
        name="ret_sample",
    )(proj, proj, proj, cos, sin, gn_g.reshape(RET_HEADS, 1, RET_DV), state)


def kernel(x_prompt, x_sample, cache_k_l0, cache_v_l0, state_ret_l1, cache_k_l2, cache_v_l2, cache_k_l3, cache_v_l3, meta_tokens, norm_g_0, w_in_0, lam_q1_0, lam_k1_0, lam_q2_0, lam_k2_0, subln_g_0, w_out_0, norm_g_1, w_in_1, gn_g_1, w_out_1, norm_g_2, w_in_2, w_out_2, norm_g_3, w_in_3, lam_q1_3, lam_k1_3, lam_q2_3, lam_k2_3, subln_g_3, w_out_3, norm_g_final):
    b, seq, d = x_prompt.shape
    db, s_len, _ = x_sample.shape
    past = cache_k_l0.shape[1]
    assert seq % FRONT_PAD == 0 and N_META <= FRONT_PAD
    lp = FRONT_PAD + seq
    n_real = N_META + seq
    first = FRONT_PAD - N_META

    meta = jnp.broadcast_to(meta_tokens.astype(F32)[None], (b, N_META, d))
    xp = jnp.concatenate([jnp.zeros((b, first, d), F32), meta, x_prompt], axis=1).reshape(b * lp, d)
    xs = x_sample.reshape(db * s_len, d)

    layers = [
        (norm_g_0, w_in_0, w_out_0, (lam_q1_0, lam_k1_0, lam_q2_0, lam_k2_0, subln_g_0), (cache_k_l0, cache_v_l0)),
        (norm_g_1, w_in_1, w_out_1, (gn_g_1,), (state_ret_l1,)),
        (norm_g_2, w_in_2, w_out_2, (), (cache_k_l2, cache_v_l2)),
        (norm_g_3, w_in_3, w_out_3, (lam_q1_3, lam_k1_3, lam_q2_3, lam_k2_3, subln_g_3), (cache_k_l3, cache_v_l3)),
    ]
    states = []
    y_prompt = y_sample = None
    n_layers = len(layers)
    for i, (norm_g, w_in, w_out, extra, cache) in enumerate(layers):
        kind = i % N_MIXERS
        w_in_b = w_in.astype(BF16)
        w_out_b = w_out.astype(BF16)
        if kind == 1:
            pp = norm_proj(xp, norm_g, w_in_b)
        else:
            if kind == 0:
                pp, kb, vt = norm_proj_kv(xp, norm_g, w_in_b, b, DA_W, DA_HEADS, BF16_ROWS)
            else:
                pp, kb, vt = norm_proj_kv(xp, norm_g, w_in_b, b, SB_W, SB_HEADS, 0)
            kb = kb.reshape(b, lp, kb.shape[-1])
        ps = norm_proj(xs, norm_g, w_in_b)
        e = w_in.shape[1]
        pp3 = pp.reshape(b, lp, e)
        ps3 = ps.reshape(db, s_len, e)
        if kind == 0:
            lam_init = 0.8 - 0.6 * math.exp(-0.3 * i)
            lam4 = jnp.stack([v.astype(F32) for v in extra[:4]])
            op = da_prompt(pp3, kb, vt, lam4, extra[4], lam_init)
            os_ = da_sample(ps3, cache[0].reshape(db, past, DA_W), cache[1].reshape(db, past, DA_W),
                            lam4, extra[4], lam_init)
            gate_block = 3
            states.append((
                pp3[:, first:, DA_W:2 * DA_W].reshape(b, n_real, 2 * DA_HEADS, DA_DH),
                pp3[:, first:, 2 * DA_W:3 * DA_W].reshape(b, n_real, DA_HEADS, DA_DV),
                ps3[:, :, DA_W:2 * DA_W].reshape(db, s_len, 2 * DA_HEADS, DA_DH),
                ps3[:, :, 2 * DA_W:3 * DA_W].reshape(db, s_len, DA_HEADS, DA_DV)))
        elif kind == 1:
            cos_p, sin_p = rope_tables(lp, -FRONT_PAD)
            cos_s, sin_s = rope_tables(s_len, past)
            op, sp = ret_prompt(pp3, cos_p, sin_p, extra[0])
            os_, ss = ret_sample(ps3, cos_s, sin_s, extra[0], cache[0])
            gate_block = 2
            states.append((sp, ss))
        else:
            op = sb_prompt(pp3, kb, vt)
            os_ = sb_sample(ps3, cache[0].reshape(db, past, SB_W), cache[1].reshape(db, past, SB_W))
            gate_block = 3
            states.append((
                pp3[:, first:, SB_W:2 * SB_W].reshape(b, n_real, SB_HEADS, SB_DH),
                pp3[:, first:, 2 * SB_W:3 * SB_W].reshape(b, n_real, SB_HEADS, SB_DH),
                ps3[:, :, SB_W:2 * SB_W].reshape(db, s_len, SB_HEADS, SB_DH),
                ps3[:, :, 2 * SB_W:3 * SB_W].reshape(db, s_len, SB_HEADS, SB_DH)))
        wdt = op.shape[-1]
        op2 = op.reshape(b * lp, wdt)
        os2 = os_.reshape(db * s_len, wdt)
        if i + 1 < n_layers:
            xp = post_proj(op2, pp, gate_block, xp, w_out_b)
            xs = post_proj(os2, ps, gate_block, xs, w_out_b)
        else:
            y_sample = post_proj(os2, ps, gate_block, xs, w_out_b, final_g=norm_g_final).reshape(db, s_len, d)
            y_prompt = post_proj(op2, pp, gate_block, xp, w_out_b, batch=b, final_g=norm_g_final,
                                 skip_rows=FRONT_PAD).reshape(b, seq, d)

    k0_p, v0_p, k0_s, v0_s = states[0]
    ret1_p, ret1_s = states[1]
    k2_p, v2_p, k2_s, v2_s = states[2]
    k3_p, v3_p, k3_s, v3_s = states[3]
    return (y_prompt, y_sample, k0_p, v0_p, k0_s, v0_s, ret1_p, ret1_s,
            k2_p, v2_p, k2_s, v2_s, k3_p, v3_p, k3_s, v3_s)
```

```python
import functools
import math

import jax
import jax.numpy as jnp
from jax import lax
from jax.experimental import pallas as pl
from jax.experimental.pallas import tpu as pltpu

F32 = jnp.float32
BF16 = jnp.bfloat16

CHUNK = 64
N_META = 16
NORM_EPS = 1e-6
N_MIXERS = 3
ROPE_BASE = 10000.0

DA_HEADS = 4
DA_DH = 64
DA_DV = 128
DA_W = 512
RET_HEADS = 4
RET_DK = 256
RET_DV = 512
SB_HEADS = 4
SB_DH = 128
SB_W = 512

LANES = 128
BF16_ROWS = 16
FRONT_PAD = 512
DA_KEYS = 768
SB_KEYS = 768
ATT_SUB = 256
DA_TQ = 512
DA_GROUP = 256
DA_LEAD = 1
SB_TQ = 512
SB_CUM = 128
SB_LINEAR = 100.0
DA_VROWS = DA_DV + BF16_ROWS
LOG2E = math.log2(math.e)
RET_BLOCK = 256
ROW_TILE = 512
VMEM_LIMIT = 48 * 1024 * 1024

NEG_INF = float("-inf")
NT_DIMS = (((1,), (1,)), ((), ()))
TN_DIMS = (((0,), (0,)), ((), ()))


def _cparams(sem):
    return pltpu.CompilerParams(dimension_semantics=sem, vmem_limit_bytes=VMEM_LIMIT)


def _silu(g):
    return g / (1.0 + jnp.exp(-g))


def _softplus(z):
    return jnp.maximum(z, 0.0) + jnp.log(1.0 + jnp.exp(-jnp.abs(z)))


def _norm_proj_kernel(x_ref, g_ref, w_ref, o_ref):
    x = x_ref[...]
    ms = jnp.mean(x * x, axis=-1, keepdims=True)
    h = (x * lax.rsqrt(ms + NORM_EPS) * g_ref[...]).astype(BF16)
    o_ref[...] = jnp.dot(h, w_ref[...], preferred_element_type=F32)


def norm_proj(x, g, w_bf16, *, col_tile=2048):
    rows, d = x.shape
    e = w_bf16.shape[1]
    tr = next(t for t in (2 * ROW_TILE, ROW_TILE, rows) if rows % t == 0)
    te = min(col_tile, e)
    return pl.pallas_call(
        _norm_proj_kernel,
        out_shape=jax.ShapeDtypeStruct((rows, e), F32),
        grid=(e // te, rows // tr),
        in_specs=[
            pl.BlockSpec((tr, d), lambda j, i: (i, 0)),
            pl.BlockSpec((1, d), lambda j, i: (0, 0)),
            pl.BlockSpec((d, te), lambda j, i: (0, j)),
        ],
        out_specs=pl.BlockSpec((tr, te), lambda j, i: (i, j)),
        compiler_params=_cparams(("parallel", "parallel")),
        name="norm_proj",
    )(x, g.reshape(1, d), w_bf16)


def _norm_proj_kv_kernel(x_ref, g_ref, w_ref, q_ref, k_ref, v_ref, gate_ref, kb_ref, vt_ref, *,
                         width, heads, ones_rows):
    x = x_ref[...]
    ms = jnp.mean(x * x, axis=-1, keepdims=True)
    h = (x * lax.rsqrt(ms + NORM_EPS) * g_ref[...]).astype(BF16)
    p = jnp.dot(h, w_ref[...], preferred_element_type=F32)
    q_ref[...] = p[:, :width]
    k_ref[...] = p[:, width:2 * width]
    v_ref[...] = p[:, 2 * width:3 * width]
    gate_ref[...] = p[:, 3 * width:]
    kb_ref[...] = p[:, width:2 * width].astype(BF16)
    vt = p[:, 2 * width:3 * width].T.astype(BF16)
    if ones_rows:
        dv = width // heads
        for hd in range(heads):
            base = hd * (dv + ones_rows)
            vt_ref[base:base + dv, :] = vt[hd * dv:(hd + 1) * dv, :]
            vt_ref[base + dv:base + dv + ones_rows, :] = jnp.ones((ones_rows, vt.shape[1]), BF16)
    else:
        vt_ref[...] = vt


def norm_proj_kv(x, g, w_bf16, batch, width, heads, ones_rows):
    rows, d = x.shape
    e = w_bf16.shape[1]
    assert e == 4 * width
    per = rows // batch
    tr = min(ROW_TILE, per)
    n_per = per // tr
    vrows = width + heads * ones_rows
    part = jax.ShapeDtypeStruct((rows, width), F32)
    part_spec = pl.BlockSpec((tr, width), lambda i: (i, 0))
    return pl.pallas_call(
        functools.partial(_norm_proj_kv_kernel, width=width, heads=heads, ones_rows=ones_rows),
        out_shape=(part, part, part, part,
                   jax.ShapeDtypeStruct((rows, width), BF16),
                   jax.ShapeDtypeStruct((batch, vrows, per), BF16)),
        grid=(rows // tr,),
        in_specs=[
            pl.BlockSpec((tr, d), lambda i: (i, 0)),
            pl.BlockSpec((1, d), lambda i: (0, 0)),
            pl.BlockSpec((d, e), lambda i: (0, 0)),
        ],
        out_specs=(part_spec, part_spec, part_spec, part_spec,
                   pl.BlockSpec((tr, width), lambda i: (i, 0)),
                   pl.BlockSpec((None, vrows, tr), lambda i: (i // n_per, 0, i % n_per))),
        compiler_params=_cparams(("parallel",)),
        name="norm_proj_kv",
    )(x, g.reshape(1, d), w_bf16)


def _post_kernel(o_ref, gate_ref, x_ref, w_ref, out_ref):
    a = (o_ref[...] * _silu(gate_ref[...])).astype(BF16)
    out_ref[...] = x_ref[...] + jnp.dot(a, w_ref[...], preferred_element_type=F32)


def _post_final_kernel(o_ref, gate_ref, x_ref, w_ref, gf_ref, out_ref):
    a = (o_ref[...] * _silu(gate_ref[...])).astype(BF16)
    xn = x_ref[...] + jnp.dot(a, w_ref[...], preferred_element_type=F32)
    ms = jnp.mean(xn * xn, axis=-1, keepdims=True)
    out_ref[...] = xn * lax.rsqrt(ms + NORM_EPS) * gf_ref[...]


def post_proj(o, proj, gate_block, x, w_bf16, *, batch=1, final_g=None, skip_rows=0):
    rows, wdt = o.shape
    d = x.shape[1]
    per = rows // batch
    tr = min(ROW_TILE, per)
    off = skip_rows // tr
    n_in = per // tr
    n = n_in - off
    in_specs = [
        pl.BlockSpec((tr, wdt), lambda bb, i: (bb * n_in + off + i, 0)),
        pl.BlockSpec((tr, wdt), lambda bb, i: (bb * n_in + off + i, gate_block)),
        pl.BlockSpec((tr, d), lambda bb, i: (bb * n_in + off + i, 0)),
        pl.BlockSpec((wdt, d), lambda bb, i: (0, 0)),
    ]
    args = [o, proj, x, w_bf16]
    kern = _post_kernel
    if final_g is not None:
        in_specs.append(pl.BlockSpec((1, d), lambda bb, i: (0, 0)))
        args.append(final_g.reshape(1, d))
        kern = _post_final_kernel
    return pl.pallas_call(
        kern,
        out_shape=jax.ShapeDtypeStruct((batch * n * tr, d), F32),
        grid=(batch, n),
        in_specs=in_specs,
        out_specs=pl.BlockSpec((tr, d), lambda bb, i: (bb * n + i, 0)),
        compiler_params=_cparams(("parallel", "parallel")),
        name="post_proj",
    )(*args)


def _rope_kernel(inv_ref, cos_ref, sin_ref, *, start):
    tr = cos_ref.shape[0]
    row = pl.program_id(0) * tr + lax.broadcasted_iota(jnp.int32, (tr, LANES), 0)
    ang = (row + start).astype(F32) * inv_ref[...]
    cos_ref[...] = jnp.cos(ang)
    sin_ref[...] = jnp.sin(ang)


def rope_tables(n, start):
    half = RET_DK // 2
    inv = (ROPE_BASE ** (-jnp.linspace(0.0, 1.0, half, dtype=F32))).reshape(1, half)
    tr = min(ROW_TILE, n)
    return pl.pallas_call(
        functools.partial(_rope_kernel, start=start),
        out_shape=(jax.ShapeDtypeStruct((n, half), F32), jax.ShapeDtypeStruct((n, half), F32)),
        grid=(n // tr,),
        in_specs=[pl.BlockSpec((1, half), lambda i: (0, 0))],
        out_specs=(pl.BlockSpec((tr, half), lambda i: (i, 0)), pl.BlockSpec((tr, half), lambda i: (i, 0))),
        compiler_params=_cparams(("parallel",)),
        name="rope_tables",
    )(inv)


def _rotary(x, cos, sin):
    half = RET_DK // 2
    x1, x2 = x[:, :half], x[:, half:]
    return jnp.concatenate([x1 * cos - x2 * sin, x1 * sin + x2 * cos], axis=1)


def _da_split_q(q):
    rows = q.shape[0]
    lane = lax.broadcasted_iota(jnp.int32, (rows, LANES), 1)
    out = []
    for h in range(DA_HEADS):
        qh = q[:, h * LANES:(h + 1) * LANES]
        qa = jnp.where(lane < DA_DH, qh, 0.0)
        qb = jnp.where(lane >= DA_DH, qh, 0.0)
        out.append(jnp.concatenate([qa, qb], axis=0).astype(BF16))
    return out


def _da_update(h, qh, kh, vh, vis, m_s, l_s, acc_s):
    s = lax.dot_general(qh, kh, NT_DIMS, preferred_element_type=F32)
    if vis is not None:
        s = jnp.where(vis, s, NEG_INF)
    m_prev = m_s[h]
    m_new = jnp.maximum(m_prev, jnp.max(s, axis=1, keepdims=True))
    m_use = jnp.where(m_new == NEG_INF, 0.0, m_new)
    alpha = jnp.exp(m_prev - m_use)
    e = jnp.exp(s - m_use)
    l_s[h] = alpha * l_s[h] + jnp.sum(e, axis=1, keepdims=True)
    acc_s[h] = alpha * acc_s[h] + jnp.dot(e.astype(BF16), vh, preferred_element_type=F32)
    m_s[h] = m_new


def _da_finish(lam_ref, g_ref, o_ref, l_s, acc_s, rows, lam_init):
    lam = (jnp.exp(jnp.sum(lam_ref[0:1, :] * lam_ref[1:2, :], axis=1, keepdims=True))
           - jnp.exp(jnp.sum(lam_ref[2:3, :] * lam_ref[3:4, :], axis=1, keepdims=True)) + lam_init)
    for h in range(DA_HEADS):
        l = l_s[h]
        o = acc_s[h] / jnp.where(l == 0.0, 1.0, l)
        d = o[:rows] - lam * o[rows:]
        ms = jnp.mean(d * d, axis=1, keepdims=True)
        o_ref[:, h * DA_DV:(h + 1) * DA_DV] = d * lax.rsqrt(ms + NORM_EPS) * g_ref[...] * (1.0 - lam_init)


def _da_prompt_kernel(qi_ref, kj_ref, q_ref, k_ref, vt_ref, lam_ref, g_ref, o_ref,
                      qs, m_s, acc_s, *, tq, lam_init):
    step = pl.program_id(1)
    qi = qi_ref[step]
    kj = kj_ref[step]
    last = (qi * tq + tq - 1) // DA_KEYS
    on_diag = (kj + 1) * DA_KEYS > qi * tq
    n_sub = DA_KEYS // ATT_SUB
    n_grp = tq // DA_GROUP
    width = 2 * DA_GROUP

    @pl.when(kj == 0)
    def _():
        q = q_ref[...] * (DA_DH ** -0.5 * LOG2E)
        row = lax.broadcasted_iota(jnp.int32, (LANES, DA_GROUP), 0)
        for g in range(n_grp):
            for h in range(DA_HEADS):
                qt = q[g * DA_GROUP:(g + 1) * DA_GROUP, h * LANES:(h + 1) * LANES].T
                qa = jnp.where(row < DA_DH, qt, 0.0)
                qb = jnp.where(row >= DA_DH, qt, 0.0)
                qs[g * DA_HEADS + h] = jnp.concatenate([qa, qb], axis=1).astype(BF16)
        m_s[...] = jnp.full(m_s.shape, NEG_INF, F32)
        acc_s[...] = jnp.zeros(acc_s.shape, F32)

    first = FRONT_PAD - N_META

    def run(mode):
        subs = range(n_sub)
        vis = {}
        if mode == "pad":
            subs = [sub for sub in subs if (sub + 1) * ATT_SUB > first]
            for sub in subs:
                if sub * ATT_SUB < first:
                    pk = sub * ATT_SUB + lax.broadcasted_iota(jnp.int32, (ATT_SUB, width), 0)
                    for g in range(n_grp):
                        vis[sub, g] = pk >= first
        elif mode == "diag":
            pk0 = kj * DA_KEYS + lax.broadcasted_iota(jnp.int32, (ATT_SUB, width), 0)
            pq0 = qi * tq + (lax.broadcasted_iota(jnp.int32, (ATT_SUB, width), 1) & (DA_GROUP - 1))
            for sub in subs:
                pk = pk0 + sub * ATT_SUB
                for g in range(n_grp):
                    vis[sub, g] = (pk >= first) & ((pk >> 6) <= ((pq0 + g * DA_GROUP) >> 6))
        items = [(sub, g, h) for sub in subs for g in range(n_grp) for h in range(DA_HEADS)]

        def scores(item):
            sub, g, h = item
            return jnp.dot(k_ref[sub * ATT_SUB:(sub + 1) * ATT_SUB, h * LANES:(h + 1) * LANES],
                           qs[g * DA_HEADS + h], preferred_element_type=F32)

        def softmax(item, s):
            sub, g, h = item
            if (sub, g) in vis:
                s = jnp.where(vis[sub, g], s, NEG_INF)
            m_prev = m_s[g * DA_HEADS + h]
            m_new = jnp.maximum(m_prev, jnp.max(s, axis=0, keepdims=True))
            m_use = jnp.where(m_new == NEG_INF, 0.0, m_new)
            m_s[g * DA_HEADS + h] = m_new
            return jnp.exp2(m_prev - m_use), jnp.exp2(s - m_use).astype(BF16)

        def accumulate(item, alpha, e):
            sub, g, h = item
            vt = vt_ref[h * DA_VROWS:(h + 1) * DA_VROWS, sub * ATT_SUB:(sub + 1) * ATT_SUB]
            acc_s[g * DA_HEADS + h] = alpha * acc_s[g * DA_HEADS + h] + jnp.dot(vt, e, preferred_element_type=F32)

        n = len(items)
        s_live, e_live = {}, {}
        for t in range(n + DA_LEAD + 1):
            if t < n:
                s_live[t] = scores(items[t])
            if DA_LEAD + 1 <= t:
                accumulate(items[t - DA_LEAD - 1], *e_live.pop(t - DA_LEAD - 1))
            if DA_LEAD <= t < n + DA_LEAD:
                e_live[t - DA_LEAD] = softmax(items[t - DA_LEAD], s_live.pop(t - DA_LEAD))

    off_diag = jnp.logical_not(on_diag)
    pl.when(on_diag)(lambda: run("diag"))
    pl.when((kj == 0) & off_diag)(lambda: run("pad"))
    pl.when((kj != 0) & off_diag)(lambda: run("open"))

    @pl.when(kj == last)
    def _():
        lam = (jnp.exp(jnp.sum(lam_ref[0:1, :] * lam_ref[1:2, :], axis=1, keepdims=True))
               - jnp.exp(jnp.sum(lam_ref[2:3, :] * lam_ref[3:4, :], axis=1, keepdims=True)) + lam_init)
        for g in range(n_grp):
            for h in range(DA_HEADS):
                acc = acc_s[g * DA_HEADS + h]
                l = acc[DA_DV:DA_DV + 1, :]
                o = acc[:DA_DV, :] / jnp.where(l == 0.0, 1.0, l)
                d = o[:, :DA_GROUP] - lam * o[:, DA_GROUP:]
                ms = jnp.mean(d * d, axis=0, keepdims=True)
                y = d * lax.rsqrt(ms + NORM_EPS) * g_ref[...] * (1.0 - lam_init)
                o_ref[g * DA_GROUP:(g + 1) * DA_GROUP, h * DA_DV:(h + 1) * DA_DV] = y.T


def _tri_steps(nq, tq, keys, descending):
    qi, kj = [], []
    for i in range(nq):
        last = (i * tq + tq - 1) // keys
        ks = range(last, -1, -1) if descending else range(last + 1)
        for k in ks:
            qi.append(i)
            kj.append(k)
    return jnp.asarray(qi, jnp.int32), jnp.asarray(kj, jnp.int32)


def da_prompt(proj, kb, vt, lam4, subln_g, lam_init):
    b, lp, _ = proj.shape
    tq = DA_TQ
    n_grp = tq // DA_GROUP
    qi, kj = _tri_steps(lp // tq, tq, DA_KEYS, descending=False)
    kern = functools.partial(_da_prompt_kernel, tq=tq, lam_init=lam_init)
    grid_spec = pltpu.PrefetchScalarGridSpec(
        num_scalar_prefetch=2,
        grid=(b, qi.shape[0]),
        in_specs=[
            pl.BlockSpec((None, tq, DA_W), lambda bb, s, qi, kj: (bb, qi[s], 0)),
            pl.BlockSpec((None, DA_KEYS, DA_W), lambda bb, s, qi, kj: (bb, kj[s], 0)),
            pl.BlockSpec((None, DA_HEADS * DA_VROWS, DA_KEYS), lambda bb, s, qi, kj: (bb, 0, kj[s])),
            pl.BlockSpec((4, DA_DH), lambda bb, s, qi, kj: (0, 0)),
            pl.BlockSpec((DA_DV, 1), lambda bb, s, qi, kj: (0, 0)),
        ],
        out_specs=pl.BlockSpec((None, tq, DA_W), lambda bb, s, qi, kj: (bb, qi[s], 0)),
        scratch_shapes=[
            pltpu.VMEM((n_grp * DA_HEADS, LANES, 2 * DA_GROUP), BF16),
            pltpu.VMEM((n_grp * DA_HEADS, 1, 2 * DA_GROUP), F32),
            pltpu.VMEM((n_grp * DA_HEADS, DA_VROWS, 2 * DA_GROUP), F32),
        ],
    )
    return pl.pallas_call(
        kern,
        out_shape=jax.ShapeDtypeStruct((b, lp, DA_W), F32),
        grid_spec=grid_spec,
        compiler_params=_cparams(("parallel", "arbitrary")),
        name="da_prompt",
    )(qi, kj, proj, kb, vt, lam4, subln_g.reshape(DA_DV, 1))


def _da_sample_kernel(q_ref, kn_ref, vn_ref, kc_ref, vc_ref, lam_ref, g_ref, o_ref,
                      qs, m_s, l_s, acc_s, *, rows, lam_init):
    j = pl.program_id(1)

    @pl.when(j == 0)
    def _():
        for h, qh in enumerate(_da_split_q(q_ref[...] * (DA_DH ** -0.5))):
            qs[h] = qh
        m_s[...] = jnp.full(m_s.shape, NEG_INF, F32)
        l_s[...] = jnp.zeros(l_s.shape, F32)
        acc_s[...] = jnp.zeros(acc_s.shape, F32)

    for h in range(DA_HEADS):
        kh = kc_ref[:, h * LANES:(h + 1) * LANES].astype(BF16)
        vh = vc_ref[:, h * DA_DV:(h + 1) * DA_DV].astype(BF16)
        _da_update(h, qs[h], kh, vh, None, m_s, l_s, acc_s)

    @pl.when(j == pl.num_programs(1) - 1)
    def _():
        for h in range(DA_HEADS):
            kh = kn_ref[:, h * LANES:(h + 1) * LANES].astype(BF16)
            vh = vn_ref[:, h * DA_DV:(h + 1) * DA_DV].astype(BF16)
            _da_update(h, qs[h], kh, vh, None, m_s, l_s, acc_s)
        _da_finish(lam_ref, g_ref, o_ref, l_s, acc_s, rows, lam_init)


def da_sample(proj, cache_k, cache_v, lam4, subln_g, lam_init):
    db, s, _ = proj.shape
    past = cache_k.shape[1]
    tk = min(1024, past)
    kern = functools.partial(_da_sample_kernel, rows=s, lam_init=lam_init)
    return pl.pallas_call(
        kern,
        out_shape=jax.ShapeDtypeStruct((db, s, DA_W), F32),
        grid=(db, past // tk),
        in_specs=[
            pl.BlockSpec((None, s, DA_W), lambda b, j: (b, 0, 0)),
            pl.BlockSpec((None, s, DA_W), lambda b, j: (b, 0, 1)),
            pl.BlockSpec((None, s, DA_W), lambda b, j: (b, 0, 2)),
            pl.BlockSpec((None, tk, DA_W), lambda b, j: (b, j, 0)),
            pl.BlockSpec((None, tk, DA_W), lambda b, j: (b, j, 0)),
            pl.BlockSpec((4, DA_DH), lambda b, j: (0, 0)),
            pl.BlockSpec((1, DA_DV), lambda b, j: (0, 0)),
        ],
        out_specs=pl.BlockSpec((None, s, DA_W), lambda b, j: (b, 0, 0)),
        scratch_shapes=[
            pltpu.VMEM((DA_HEADS, 2 * s, LANES), BF16),
            pltpu.VMEM((DA_HEADS, 2 * s, 1), F32),
            pltpu.VMEM((DA_HEADS, 2 * s, 1), F32),
            pltpu.VMEM((DA_HEADS, 2 * s, DA_DV), F32),
        ],
        compiler_params=_cparams(("parallel", "arbitrary")),
        name="da_sample",
    )(proj, proj, proj, cache_k, cache_v, lam4, subln_g.reshape(1, DA_DV))


def _sb_update(h, qh, kh, vh, u, vis, r_s, acc_s):
    z = lax.dot_general(qh, kh, NT_DIMS, preferred_element_type=F32)
    lk = -_softplus(z)
    if vis is not None:
        lk = jnp.where(vis, lk, 0.0)
    hi = lk.astype(BF16)
    lo = (lk - hi.astype(F32)).astype(BF16)
    tail = (jnp.dot(hi, u, preferred_element_type=F32) + jnp.dot(lo, u, preferred_element_type=F32)) + r_s[h]
    a = jnp.exp(z + tail)
    if vis is not None:
        a = jnp.where(vis, a, 0.0)
    acc_s[h] = acc_s[h] + jnp.dot(a.astype(BF16), vh, preferred_element_type=F32)
    r_s[h] = r_s[h] + jnp.sum(lk, axis=1, keepdims=True)


def _sb_prompt_kernel(qi_ref, kj_ref, q_ref, k_ref, vt_ref, w_ref, o_ref, qs, r_s, acc_s, *, tq):
    step = pl.program_id(1)
    qi = qi_ref[step]
    kj = kj_ref[step]
    last = (qi * tq + tq - 1) // SB_KEYS
    on_diag = (kj + 1) * SB_KEYS > qi * tq
    n_sub = SB_KEYS // ATT_SUB

    @pl.when(kj == last)
    def _():
        q = q_ref[...] * (SB_DH ** -0.5 * LOG2E)
        for h in range(SB_HEADS):
            qs[h] = q[:, h * SB_DH:(h + 1) * SB_DH].T.astype(BF16)
        r_s[...] = jnp.zeros(r_s.shape, F32)
        acc_s[...] = jnp.zeros(acc_s.shape, F32)

    def run(masked):
        items = [(sub, h) for sub in range(n_sub - 1, -1, -1) for h in range(SB_HEADS)]
        if masked:
            pk0 = kj * SB_KEYS + lax.broadcasted_iota(jnp.int32, (ATT_SUB, tq), 0)
            pq = qi * tq + lax.broadcasted_iota(jnp.int32, (ATT_SUB, tq), 1)
            vis = [pk0 + sub * ATT_SUB < pq for sub in range(n_sub)]
        w2 = w_ref[...]
        n_blk = ATT_SUB // SB_CUM

        def logits(item):
            sub, h = item
            return jnp.dot(k_ref[sub * ATT_SUB:(sub + 1) * ATT_SUB, h * SB_DH:(h + 1) * SB_DH], qs[h],
                           preferred_element_type=F32)

        def split(item, z):
            sub, _ = item
            sp = jnp.where(z > SB_LINEAR, z, jnp.log(1.0 + jnp.exp2(z)) * LOG2E)
            if masked:
                sp = jnp.where(vis[sub], sp, 0.0)
            hi = sp.astype(BF16)
            return hi, (sp - hi.astype(F32)).astype(BF16)

        def suffix(hi, lo):
            return [jnp.dot(w2, jnp.concatenate([hi[b * SB_CUM:(b + 1) * SB_CUM], lo[b * SB_CUM:(b + 1) * SB_CUM]],
                                                axis=0), preferred_element_type=F32) for b in range(n_blk)]

        def weights(item, z, tins):
            sub, h = item
            r = r_s[h]
            tails = [None] * n_blk
            for b in range(n_blk - 1, -1, -1):
                tails[b] = tins[b] + r
                r = r + tins[b][0:1, :]
            r_s[h] = r
            a = jnp.exp2(z - jnp.concatenate(tails, axis=0))
            if masked:
                a = jnp.where(vis[sub], a, 0.0)
            return a.astype(BF16)

        def accumulate(item, a):
            sub, h = item
            vt = vt_ref[h * SB_DH:(h + 1) * SB_DH, sub * ATT_SUB:(sub + 1) * ATT_SUB]
            acc_s[h] = acc_s[h] + jnp.dot(vt, a, preferred_element_type=F32)

        n = len(items)
        z_live, t_live = {}, {}
        for t in range(n + 2):
            if t < n:
                z_live[t] = logits(items[t])
            if 1 <= t < n + 1:
                t_live[t - 1] = suffix(*split(items[t - 1], z_live[t - 1]))
            if t >= 2:
                accumulate(items[t - 2], weights(items[t - 2], z_live.pop(t - 2), t_live.pop(t - 2)))

    pl.when(on_diag)(lambda: run(True))
    pl.when(jnp.logical_not(on_diag))(lambda: run(False))

    @pl.when(kj == 0)
    def _():
        for h in range(SB_HEADS):
            o_ref[:, h * SB_DH:(h + 1) * SB_DH] = acc_s[h].T


def _tri_ge(n, transpose=False):
    idx = jnp.arange(n)
    m = idx[:, None] >= idx[None, :]
    return (m.T if transpose else m).astype(BF16)


def sb_prompt(proj, kb, vt):
    b, lp, _ = proj.shape
    tq = SB_TQ
    qi, kj = _tri_steps(lp // tq, tq, SB_KEYS, descending=True)
    kern = functools.partial(_sb_prompt_kernel, tq=tq)
    grid_spec = pltpu.PrefetchScalarGridSpec(
        num_scalar_prefetch=2,
        grid=(b, qi.shape[0]),
        in_specs=[
            pl.BlockSpec((None, tq, SB_W), lambda bb, s, qi, kj: (bb, qi[s], 0)),
            pl.BlockSpec((None, SB_KEYS, SB_W), lambda bb, s, qi, kj: (bb, kj[s], 0)),
            pl.BlockSpec((None, SB_W, SB_KEYS), lambda bb, s, qi, kj: (bb, 0, kj[s])),
            pl.BlockSpec((SB_CUM, 2 * SB_CUM), lambda bb, s, qi, kj: (0, 0)),
        ],
        out_specs=pl.BlockSpec((None, tq, SB_W), lambda bb, s, qi, kj: (bb, qi[s], 0)),
        scratch_shapes=[
            pltpu.VMEM((SB_HEADS, SB_DH, tq), BF16),
            pltpu.VMEM((SB_HEADS, 1, tq), F32),
            pltpu.VMEM((SB_HEADS, SB_DH, tq), F32),
        ],
    )
    return pl.pallas_call(
        kern,
        out_shape=jax.ShapeDtypeStruct((b, lp, SB_W), F32),
        grid_spec=grid_spec,
        compiler_params=_cparams(("parallel", "arbitrary")),
        name="sb_prompt",
    )(qi, kj, proj, kb, vt, jnp.tile(_tri_ge(SB_CUM, transpose=True), (1, 2)))


def _sb_sample_kernel(q_ref, kn_ref, vn_ref, kc_ref, vc_ref, un_ref, uc_ref, o_ref, qs, r_s, acc_s, *, rows):
    j = pl.program_id(1)

    @pl.when(j == 0)
    def _():
        q = q_ref[...] * (SB_DH ** -0.5)
        for h in range(SB_HEADS):
            qs[h] = q[:, h * SB_DH:(h + 1) * SB_DH].astype(BF16)
        r_s[...] = jnp.zeros(r_s.shape, F32)
        acc_s[...] = jnp.zeros(acc_s.shape, F32)
        iq = lax.broadcasted_iota(jnp.int32, (rows, rows), 0)
        ik = lax.broadcasted_iota(jnp.int32, (rows, rows), 1)
        vis = ik < iq
        un = un_ref[...]
        for h in range(SB_HEADS):
            kh = kn_ref[:, h * SB_DH:(h + 1) * SB_DH].astype(BF16)
            vh = vn_ref[:, h * SB_DH:(h + 1) * SB_DH].astype(BF16)
            _sb_update(h, qs[h], kh, vh, un, vis, r_s, acc_s)

    uc = uc_ref[...]
    for h in range(SB_HEADS):
        kh = kc_ref[:, h * SB_DH:(h + 1) * SB_DH].astype(BF16)
        vh = vc_ref[:, h * SB_DH:(h + 1) * SB_DH].astype(BF16)
        _sb_update(h, qs[h], kh, vh, uc, None, r_s, acc_s)

    @pl.when(j == pl.num_programs(1) - 1)
    def _():
        for h in range(SB_HEADS):
            o_ref[:, h * SB_DH:(h + 1) * SB_DH] = acc_s[h]


def sb_sample(proj, cache_k, cache_v):
    db, s, _ = proj.shape
    past = cache_k.shape[1]
    tk = min(512, past)
    n = past // tk
    kern = functools.partial(_sb_sample_kernel, rows=s)
    return pl.pallas_call(
        kern,
        out_shape=jax.ShapeDtypeStruct((db, s, SB_W), F32),
        grid=(db, n),
        in_specs=[
            pl.BlockSpec((None, s, SB_W), lambda b, j: (b, 0, 0)),
            pl.BlockSpec((None, s, SB_W), lambda b, j: (b, 0, 1)),
            pl.BlockSpec((None, s, SB_W), lambda b, j: (b, 0, 2)),
            pl.BlockSpec((None, tk, SB_W), lambda b, j: (b, n - 1 - j, 0)),
            pl.BlockSpec((None, tk, SB_W), lambda b, j: (b, n - 1 - j, 0)),
            pl.BlockSpec((s, s), lambda b, j: (0, 0)),
            pl.BlockSpec((tk, tk), lambda b, j: (0, 0)),
        ],
        out_specs=pl.BlockSpec((None, s, SB_W), lambda b, j: (b, 0, 0)),
        scratch_shapes=[
            pltpu.VMEM((SB_HEADS, s, SB_DH), BF16),
            pltpu.VMEM((SB_HEADS, s, 1), F32),
            pltpu.VMEM((SB_HEADS, s, SB_DH), F32),
        ],
        compiler_params=_cparams(("parallel", "arbitrary")),
        name="sb_sample",
    )(proj, proj, proj, cache_k, cache_v, _tri_ge(s), _tri_ge(tk))


def _ret_block(q, k, v, cos, sin, state, log_gamma):
    c = q.shape[0]
    qr = _rotary(q, cos, sin)
    kr = _rotary(k, cos, sin) * (RET_DK ** -0.5)
    ii = lax.broadcasted_iota(jnp.int32, (c, c), 0)
    jj = lax.broadcasted_iota(jnp.int32, (c, c), 1)
    rel = (ii - jj).astype(F32)
    decay = jnp.where(rel >= 0.0, jnp.exp(jnp.maximum(rel, 0.0) * log_gamma), 0.0)
    idx = lax.broadcasted_iota(jnp.int32, (c, 1), 0).astype(F32)
    q_dec = (qr * jnp.exp((idx + 1.0) * log_gamma)).astype(BF16)
    k_dec = (kr * jnp.exp((c - 1.0 - idx) * log_gamma)).astype(BF16)
    vb = v.astype(BF16)
    scores = lax.dot_general(qr.astype(BF16), kr.astype(BF16), NT_DIMS, preferred_element_type=F32) * decay
    o = (jnp.dot(scores.astype(BF16), vb, preferred_element_type=F32)
         + jnp.dot(q_dec, state.astype(BF16), preferred_element_type=F32))
    new_state = (jnp.exp(c * log_gamma) * state
                 + lax.dot_general(k_dec, vb, TN_DIMS, preferred_element_type=F32))
    return o, new_state


def _ret_norm(o, g):
    ms = jnp.mean(o * o, axis=1, keepdims=True)
    return o * lax.rsqrt(ms + NORM_EPS) * g


def _log_gamma(h):
    hv = jnp.zeros((1, 1), F32) + (float(h) if isinstance(h, int) else h.astype(F32))
    return jnp.log(1.0 - jnp.exp2(-5.0 - hv))


def _ret_prompt_kernel(q_ref, k_ref, v_ref, cos_ref, sin_ref, g_ref, o_ref, sf_ref, st):
    j = pl.program_id(1)
    c = q_ref.shape[0]

    @pl.when(j == 0)
    def _():
        st[...] = jnp.zeros(st.shape, F32)

    cos, sin = cos_ref[...], sin_ref[...]
    ii = lax.broadcasted_iota(jnp.int32, (c, c), 0)
    jj = lax.broadcasted_iota(jnp.int32, (c, c), 1)
    rel = jnp.maximum(ii - jj, 0).astype(F32)
    idx = lax.broadcasted_iota(jnp.int32, (c, 1), 0).astype(F32)
    heads = range(RET_HEADS)
    lg = [_log_gamma(h) for h in heads]
    qr = [_rotary(q_ref[:, h * RET_DK:(h + 1) * RET_DK], cos, sin) for h in heads]
    kr = [_rotary(k_ref[:, h * RET_DK:(h + 1) * RET_DK], cos, sin) * (RET_DK ** -0.5) for h in heads]
    vb = [v_ref[:, h * RET_DV:(h + 1) * RET_DV].astype(BF16) for h in heads]
    scores = [lax.dot_general(qr[h].astype(BF16), kr[h].astype(BF16), NT_DIMS, preferred_element_type=F32)
              for h in heads]
    cross = [jnp.dot((qr[h] * jnp.exp((idx + 1.0) * lg[h])).astype(BF16), st[h].astype(BF16),
                     preferred_element_type=F32) for h in heads]
    grown = [lax.dot_general((kr[h] * jnp.exp((c - 1.0 - idx) * lg[h])).astype(BF16), vb[h], TN_DIMS,
                             preferred_element_type=F32) for h in heads]
    for h in heads:
        decay = jnp.where(ii >= jj, jnp.exp(rel * lg[h]), 0.0)
        o = jnp.dot((scores[h] * decay).astype(BF16), vb[h], preferred_element_type=F32) + cross[h]
        o_ref[:, h * RET_DV:(h + 1) * RET_DV] = _ret_norm(o, g_ref[h])
        st[h] = jnp.exp(c * lg[h]) * st[h] + grown[h]

    @pl.when(j == pl.num_programs(1) - 1)
    def _():
        sf_ref[...] = st[...]


def ret_prompt(proj, cos, sin, gn_g):
    b, lp, _ = proj.shape
    c = RET_BLOCK
    qk_w = RET_HEADS * RET_DK
    v_w = RET_HEADS * RET_DV
    return pl.pallas_call(
        _ret_prompt_kernel,
        out_shape=(jax.ShapeDtypeStruct((b, lp, v_w), F32),
                   jax.ShapeDtypeStruct((b, RET_HEADS, RET_DK, RET_DV), F32)),
        grid=(b, lp // c),
        in_specs=[
            pl.BlockSpec((None, c, qk_w), lambda bb, j: (bb, j, 0)),
            pl.BlockSpec((None, c, qk_w), lambda bb, j: (bb, j, 1)),
            pl.BlockSpec((None, c, v_w), lambda bb, j: (bb, j, 1)),
            pl.BlockSpec((c, RET_DK // 2), lambda bb, j: (j, 0)),
            pl.BlockSpec((c, RET_DK // 2), lambda bb, j: (j, 0)),
            pl.BlockSpec((RET_HEADS, 1, RET_DV), lambda bb, j: (0, 0, 0)),
        ],
        out_specs=(pl.BlockSpec((None, c, v_w), lambda bb, j: (bb, j, 0)),
                   pl.BlockSpec((None, RET_HEADS, RET_DK, RET_DV), lambda bb, j: (bb, 0, 0, 0))),
        scratch_shapes=[pltpu.VMEM((RET_HEADS, RET_DK, RET_DV), F32)],
        compiler_params=_cparams(("parallel", "arbitrary")),
        name="ret_prompt",
    )(proj, proj, proj, cos, sin, gn_g.reshape(RET_HEADS, 1, RET_DV))


def _ret_sample_kernel(q_ref, k_ref, v_ref, cos_ref, sin_ref, g_ref, s_ref, o_ref, sf_ref):
    o, new_state = _ret_block(q_ref[...], k_ref[...], v_ref[...], cos_ref[...], sin_ref[...], s_ref[...],
                              _log_gamma(pl.program_id(1)))
    o_ref[...] = _ret_norm(o, g_ref[...])
    sf_ref[...] = new_state


def ret_sample(proj, cos, sin, gn_g, state):
    db, s, _ = proj.shape
    kv_off = (RET_HEADS * RET_DK) // RET_DK
    v_off = (2 * RET_HEADS * RET_DK) // RET_DV
    return pl.pallas_call(
        _ret_sample_kernel,
        out_shape=(jax.ShapeDtypeStruct((db, s, RET_HEADS * RET_DV), F32),
                   jax.ShapeDtypeStruct(state.shape, F32)),
        grid=(db, RET_HEADS),
        in_specs=[
            pl.BlockSpec((None, s, RET_DK), lambda b, h: (b, 0, h)),
            pl.BlockSpec((None, s, RET_DK), lambda b, h: (b, 0, kv_off + h)),
            pl.BlockSpec((None, s, RET_DV), lambda b, h: (b, 0, v_off + h)),
            pl.BlockSpec((s, RET_DK // 2), lambda b, h: (0, 0)),
            pl.BlockSpec((s, RET_DK // 2), lambda b, h: (0, 0)),
            pl.BlockSpec((None, 1, RET_DV), lambda b, h: (h, 0, 0)),
            pl.BlockSpec((None, None, RET_DK, RET_DV), lambda b, h: (b, h, 0, 0)),
        ],
        out_specs=(pl.BlockSpec((None, s, RET_DV), lambda b, h: (b, 0, h)),
                   pl.BlockSpec((None, None, RET_DK, RET_DV), lambda b, h: (b, h, 0, 0))),
        compiler_params=_cparams(("parallel", "parallel")),
        name="ret_sample",
    )(proj, proj, proj, cos, sin, gn_g.reshape(RET_HEADS, 1, RET_DV), state)


def kernel(x_prompt, x_sample, cache_k_l0, cache_v_l0, state_ret_l1, cache_k_l2, cache_v_l2, cache_k_l3, cache_v_l3, meta_tokens, norm_g_0, w_in_0, lam_q1_0, lam_k1_0, lam_q2_0, lam_k2_0, subln_g_0, w_out_0, norm_g_1, w_in_1, gn_g_1, w_out_1, norm_g_2, w_in_2, w_out_2, norm_g_3, w_in_3, lam_q1_3, lam_k1_3, lam_q2_3, lam_k2_3, subln_g_3, w_out_3, norm_g_final):
    b, seq, d = x_prompt.shape
    db, s_len, _ = x_sample.shape
    past = cache_k_l0.shape[1]
    assert seq % FRONT_PAD == 0 and N_META <= FRONT_PAD
    lp = FRONT_PAD + seq
    n_real = N_META + seq
    first = FRONT_PAD - N_META

    meta = jnp.broadcast_to(meta_tokens.astype(F32)[None], (b, N_META, d))
    xp = jnp.concatenate([jnp.zeros((b, first, d), F32), meta, x_prompt], axis=1).reshape(b * lp, d)
    xs = x_sample.reshape(db * s_len, d)

    layers = [
        (norm_g_0, w_in_0, w_out_0, (lam_q1_0, lam_k1_0, lam_q2_0, lam_k2_0, subln_g_0), (cache_k_l0, cache_v_l0)),
        (norm_g_1, w_in_1, w_out_1, (gn_g_1,), (state_ret_l1,)),
        (norm_g_2, w_in_2, w_out_2, (), (cache_k_l2, cache_v_l2)),
        (norm_g_3, w_in_3, w_out_3, (lam_q1_3, lam_k1_3, lam_q2_3, lam_k2_3, subln_g_3), (cache_k_l3, cache_v_l3)),
    ]
    states = []
    y_prompt = y_sample = None
    n_layers = len(layers)
    for i, (norm_g, w_in, w_out, extra, cache) in enumerate(layers):
        kind = i % N_MIXERS
        w_in_b = w_in.astype(BF16)
        w_out_b = w_out.astype(BF16)
        ps = norm_proj(xs, norm_g, w_in_b)
        e = w_in.shape[1]
        ps3 = ps.reshape(db, s_len, e)
        if kind == 1:
            pp = norm_proj(xp, norm_g, w_in_b)
            gate_p, gate_block_p, gate_block_s = pp, 2, 2
            cos_p, sin_p = rope_tables(lp, -FRONT_PAD)
            cos_s, sin_s = rope_tables(s_len, past)
            op, sp = ret_prompt(pp.reshape(b, lp, e), cos_p, sin_p, extra[0])
            os_, ss = ret_sample(ps3, cos_s, sin_s, extra[0], cache[0])
            states.append((sp, ss))
        else:
            wdt, heads = (DA_W, DA_HEADS) if kind == 0 else (SB_W, SB_HEADS)
            qp, kp, vp, gate_p, kb, vt = norm_proj_kv(xp, norm_g, w_in_b, b, wdt, heads,
                                                      BF16_ROWS if kind == 0 else 0)
            gate_block_p, gate_block_s = 0, 3
            q3 = qp.reshape(b, lp, wdt)
            kb = kb.reshape(b, lp, wdt)
            ck = cache[0].reshape(db, past, wdt)
            cv = cache[1].reshape(db, past, wdt)
            if kind == 0:
                lam_init = 0.8 - 0.6 * math.exp(-0.3 * i)
                lam4 = jnp.stack([v.astype(F32) for v in extra[:4]])
                op = da_prompt(q3, kb, vt, lam4, extra[4], lam_init)
                os_ = da_sample(ps3, ck, cv, lam4, extra[4], lam_init)
                k_shape, v_shape = (2 * DA_HEADS, DA_DH), (DA_HEADS, DA_DV)
            else:
                op = sb_prompt(q3, kb, vt)
                os_ = sb_sample(ps3, ck, cv)
                k_shape, v_shape = (SB_HEADS, SB_DH), (SB_HEADS, SB_DH)
            states.append((
                kp.reshape(b, lp, wdt)[:, first:].reshape(b, n_real, *k_shape),
                vp.reshape(b, lp, wdt)[:, first:].reshape(b, n_real, *v_shape),
                ps3[:, :, wdt:2 * wdt].reshape(db, s_len, *k_shape),
                ps3[:, :, 2 * wdt:3 * wdt].reshape(db, s_len, *v_shape)))
        wdt = op.shape[-1]
        op2 = op.reshape(b * lp, wdt)
        os2 = os_.reshape(db * s_len, wdt)
        if i + 1 < n_layers:
            xp = post_proj(op2, gate_p, gate_block_p, xp, w_out_b)
            xs = post_proj(os2, ps, gate_block_s, xs, w_out_b)
        else:
            y_sample = post_proj(os2, ps, gate_block_s, xs, w_out_b, final_g=norm_g_final).reshape(db, s_len, d)
            y_prompt = post_proj(op2, gate_p, gate_block_p, xp, w_out_b, batch=b, final_g=norm_g_final,
                                 skip_rows=FRONT_PAD).reshape(b, seq, d)

    k0_p, v0_p, k0_s, v0_s = states[0]
    ret1_p, ret1_s = states[1]
    k2_p, v2_p, k2_s, v2_s = states[2]
    k3_p, v3_p, k3_s, v3_s = states[3]
    return (y_prompt, y_sample, k0_p, v0_p, k0_s, v0_s, ret1_p, ret1_s,
            k2_p, v2_p, k2_s, v2_s, k3_p, v3_p, k3_s, v3_s)
```

```python
import functools
import math

import jax
import jax.numpy as jnp
from jax import lax
from jax.experimental import pallas as pl
from jax.experimental.pallas import tpu as pltpu

F32 = jnp.float32
BF16 = jnp.bfloat16

CHUNK = 64
N_META = 16
NORM_EPS = 1e-6
N_MIXERS = 3
ROPE_BASE = 10000.0

DA_HEADS = 4
DA_DH = 64
DA_DV = 128
DA_W = 512
RET_HEADS = 4
RET_DK = 256
RET_DV = 512
SB_HEADS = 4
SB_DH = 128
SB_W = 512

LANES = 128
BF16_ROWS = 16
FRONT_PAD = 512
DA_KEYS = 768
SB_KEYS = 768
ATT_SUB = 256
DA_TQ = 512
DA_GROUP = 256
DA_LEAD = 1
SB_TQ = 512
SB_CUM = 128
SB_LINEAR = 100.0
SB_ITEM = 128
DA_VROWS = DA_DV + BF16_ROWS
LOG2E = math.log2(math.e)
RET_BLOCK = 256
ROW_TILE = 512
VMEM_LIMIT = 48 * 1024 * 1024

NEG_INF = float("-inf")
NT_DIMS = (((1,), (1,)), ((), ()))
TN_DIMS = (((0,), (0,)), ((), ()))


def _cparams(sem):
    return pltpu.CompilerParams(dimension_semantics=sem, vmem_limit_bytes=VMEM_LIMIT)


def _silu(g):
    return g / (1.0 + jnp.exp(-g))


def _softplus(z):
    return jnp.maximum(z, 0.0) + jnp.log(1.0 + jnp.exp(-jnp.abs(z)))


def _norm_proj_kernel(x_ref, g_ref, w_ref, o_ref):
    x = x_ref[...]
    ms = jnp.mean(x * x, axis=-1, keepdims=True)
    h = (x * lax.rsqrt(ms + NORM_EPS) * g_ref[...]).astype(BF16)
    o_ref[...] = jnp.dot(h, w_ref[...], preferred_element_type=F32)


def norm_proj(x, g, w_bf16, *, col_tile=2048):
    rows, d = x.shape
    e = w_bf16.shape[1]
    tr = next(t for t in (2 * ROW_TILE, ROW_TILE, rows) if rows % t == 0)
    te = min(col_tile, e)
    return pl.pallas_call(
        _norm_proj_kernel,
        out_shape=jax.ShapeDtypeStruct((rows, e), F32),
        grid=(e // te, rows // tr),
        in_specs=[
            pl.BlockSpec((tr, d), lambda j, i: (i, 0)),
            pl.BlockSpec((1, d), lambda j, i: (0, 0)),
            pl.BlockSpec((d, te), lambda j, i: (0, j)),
        ],
        out_specs=pl.BlockSpec((tr, te), lambda j, i: (i, j)),
        compiler_params=_cparams(("parallel", "parallel")),
        name="norm_proj",
    )(x, g.reshape(1, d), w_bf16)


def _norm_proj_kv_kernel(x_ref, g_ref, w_ref, q_ref, k_ref, v_ref, gate_ref, kb_ref, vt_ref, *,
                         width, heads, ones_rows):
    x = x_ref[...]
    ms = jnp.mean(x * x, axis=-1, keepdims=True)
    h = (x * lax.rsqrt(ms + NORM_EPS) * g_ref[...]).astype(BF16)
    p = jnp.dot(h, w_ref[...], preferred_element_type=F32)
    q_ref[...] = p[:, :width]
    k_ref[...] = p[:, width:2 * width]
    v_ref[...] = p[:, 2 * width:3 * width]
    gate_ref[...] = p[:, 3 * width:]
    kb_ref[...] = p[:, width:2 * width].astype(BF16)
    vt = p[:, 2 * width:3 * width].T.astype(BF16)
    if ones_rows:
        dv = width // heads
        for hd in range(heads):
            base = hd * (dv + ones_rows)
            vt_ref[base:base + dv, :] = vt[hd * dv:(hd + 1) * dv, :]
            vt_ref[base + dv:base + dv + ones_rows, :] = jnp.ones((ones_rows, vt.shape[1]), BF16)
    else:
        vt_ref[...] = vt


def norm_proj_kv(x, g, w_bf16, batch, width, heads, ones_rows):
    rows, d = x.shape
    e = w_bf16.shape[1]
    assert e == 4 * width
    per = rows // batch
    tr = min(ROW_TILE, per)
    n_per = per // tr
    vrows = width + heads * ones_rows
    part = jax.ShapeDtypeStruct((rows, width), F32)
    part_spec = pl.BlockSpec((tr, width), lambda i: (i, 0))
    return pl.pallas_call(
        functools.partial(_norm_proj_kv_kernel, width=width, heads=heads, ones_rows=ones_rows),
        out_shape=(part, part, part, part,
                   jax.ShapeDtypeStruct((rows, width), BF16),
                   jax.ShapeDtypeStruct((batch, vrows, per), BF16)),
        grid=(rows // tr,),
        in_specs=[
            pl.BlockSpec((tr, d), lambda i: (i, 0)),
            pl.BlockSpec((1, d), lambda i: (0, 0)),
            pl.BlockSpec((d, e), lambda i: (0, 0)),
        ],
        out_specs=(part_spec, part_spec, part_spec, part_spec,
                   pl.BlockSpec((tr, width), lambda i: (i, 0)),
                   pl.BlockSpec((None, vrows, tr), lambda i: (i // n_per, 0, i % n_per))),
        compiler_params=_cparams(("parallel",)),
        name="norm_proj_kv",
    )(x, g.reshape(1, d), w_bf16)


def _post_kernel(o_ref, gate_ref, x_ref, w_ref, out_ref):
    a = (o_ref[...] * _silu(gate_ref[...])).astype(BF16)
    out_ref[...] = x_ref[...] + jnp.dot(a, w_ref[...], preferred_element_type=F32)


def _post_final_kernel(o_ref, gate_ref, x_ref, w_ref, gf_ref, out_ref):
    a = (o_ref[...] * _silu(gate_ref[...])).astype(BF16)
    xn = x_ref[...] + jnp.dot(a, w_ref[...], preferred_element_type=F32)
    ms = jnp.mean(xn * xn, axis=-1, keepdims=True)
    out_ref[...] = xn * lax.rsqrt(ms + NORM_EPS) * gf_ref[...]


def post_proj(o, proj, gate_block, x, w_bf16, *, batch=1, final_g=None, skip_rows=0):
    rows, wdt = o.shape
    d = x.shape[1]
    per = rows // batch
    tr = min(ROW_TILE, per)
    off = skip_rows // tr
    n_in = per // tr
    n = n_in - off
    in_specs = [
        pl.BlockSpec((tr, wdt), lambda bb, i: (bb * n_in + off + i, 0)),
        pl.BlockSpec((tr, wdt), lambda bb, i: (bb * n_in + off + i, gate_block)),
        pl.BlockSpec((tr, d), lambda bb, i: (bb * n_in + off + i, 0)),
        pl.BlockSpec((wdt, d), lambda bb, i: (0, 0)),
    ]
    args = [o, proj, x, w_bf16]
    kern = _post_kernel
    if final_g is not None:
        in_specs.append(pl.BlockSpec((1, d), lambda bb, i: (0, 0)))
        args.append(final_g.reshape(1, d))
        kern = _post_final_kernel
    return pl.pallas_call(
        kern,
        out_shape=jax.ShapeDtypeStruct((batch * n * tr, d), F32),
        grid=(batch, n),
        in_specs=in_specs,
        out_specs=pl.BlockSpec((tr, d), lambda bb, i: (bb * n + i, 0)),
        compiler_params=_cparams(("parallel", "parallel")),
        name="post_proj",
    )(*args)


def _rope_kernel(inv_ref, cos_ref, sin_ref, *, start):
    tr = cos_ref.shape[0]
    row = pl.program_id(0) * tr + lax.broadcasted_iota(jnp.int32, (tr, LANES), 0)
    ang = (row + start).astype(F32) * inv_ref[...]
    cos_ref[...] = jnp.cos(ang)
    sin_ref[...] = jnp.sin(ang)


def rope_tables(n, start):
    half = RET_DK // 2
    inv = (ROPE_BASE ** (-jnp.linspace(0.0, 1.0, half, dtype=F32))).reshape(1, half)
    tr = min(ROW_TILE, n)
    return pl.pallas_call(
        functools.partial(_rope_kernel, start=start),
        out_shape=(jax.ShapeDtypeStruct((n, half), F32), jax.ShapeDtypeStruct((n, half), F32)),
        grid=(n // tr,),
        in_specs=[pl.BlockSpec((1, half), lambda i: (0, 0))],
        out_specs=(pl.BlockSpec((tr, half), lambda i: (i, 0)), pl.BlockSpec((tr, half), lambda i: (i, 0))),
        compiler_params=_cparams(("parallel",)),
        name="rope_tables",
    )(inv)


def _rotary(x, cos, sin):
    half = RET_DK // 2
    x1, x2 = x[:, :half], x[:, half:]
    return jnp.concatenate([x1 * cos - x2 * sin, x1 * sin + x2 * cos], axis=1)


def _da_split_q(q):
    rows = q.shape[0]
    lane = lax.broadcasted_iota(jnp.int32, (rows, LANES), 1)
    out = []
    for h in range(DA_HEADS):
        qh = q[:, h * LANES:(h + 1) * LANES]
        qa = jnp.where(lane < DA_DH, qh, 0.0)
        qb = jnp.where(lane >= DA_DH, qh, 0.0)
        out.append(jnp.concatenate([qa, qb], axis=0).astype(BF16))
    return out


def _da_update(h, qh, kh, vh, vis, m_s, l_s, acc_s):
    s = lax.dot_general(qh, kh, NT_DIMS, preferred_element_type=F32)
    if vis is not None:
        s = jnp.where(vis, s, NEG_INF)
    m_prev = m_s[h]
    m_new = jnp.maximum(m_prev, jnp.max(s, axis=1, keepdims=True))
    m_use = jnp.where(m_new == NEG_INF, 0.0, m_new)
    alpha = jnp.exp(m_prev - m_use)
    e = jnp.exp(s - m_use)
    l_s[h] = alpha * l_s[h] + jnp.sum(e, axis=1, keepdims=True)
    acc_s[h] = alpha * acc_s[h] + jnp.dot(e.astype(BF16), vh, preferred_element_type=F32)
    m_s[h] = m_new


def _da_finish(lam_ref, g_ref, o_ref, l_s, acc_s, rows, lam_init):
    lam = (jnp.exp(jnp.sum(lam_ref[0:1, :] * lam_ref[1:2, :], axis=1, keepdims=True))
           - jnp.exp(jnp.sum(lam_ref[2:3, :] * lam_ref[3:4, :], axis=1, keepdims=True)) + lam_init)
    for h in range(DA_HEADS):
        l = l_s[h]
        o = acc_s[h] / jnp.where(l == 0.0, 1.0, l)
        d = o[:rows] - lam * o[rows:]
        ms = jnp.mean(d * d, axis=1, keepdims=True)
        o_ref[:, h * DA_DV:(h + 1) * DA_DV] = d * lax.rsqrt(ms + NORM_EPS) * g_ref[...] * (1.0 - lam_init)


def _da_prompt_kernel(qi_ref, kj_ref, q_ref, k_ref, vt_ref, lam_ref, g_ref, o_ref,
                      qs, m_s, acc_s, *, tq, lam_init):
    step = pl.program_id(1)
    qi = qi_ref[step]
    kj = kj_ref[step]
    last = (qi * tq + tq - 1) // DA_KEYS
    on_diag = (kj + 1) * DA_KEYS > qi * tq
    n_sub = DA_KEYS // ATT_SUB
    n_grp = tq // DA_GROUP
    width = 2 * DA_GROUP

    @pl.when(kj == 0)
    def _():
        q = q_ref[...] * (DA_DH ** -0.5 * LOG2E)
        row = lax.broadcasted_iota(jnp.int32, (LANES, DA_GROUP), 0)
        for g in range(n_grp):
            for h in range(DA_HEADS):
                qt = q[g * DA_GROUP:(g + 1) * DA_GROUP, h * LANES:(h + 1) * LANES].T
                qa = jnp.where(row < DA_DH, qt, 0.0)
                qb = jnp.where(row >= DA_DH, qt, 0.0)
                qs[g * DA_HEADS + h] = jnp.concatenate([qa, qb], axis=1).astype(BF16)
        m_s[...] = jnp.full(m_s.shape, NEG_INF, F32)
        acc_s[...] = jnp.zeros(acc_s.shape, F32)

    first = FRONT_PAD - N_META

    def run(mode):
        subs = range(n_sub)
        vis = {}
        if mode == "pad":
            subs = [sub for sub in subs if (sub + 1) * ATT_SUB > first]
            for sub in subs:
                if sub * ATT_SUB < first:
                    pk = sub * ATT_SUB + lax.broadcasted_iota(jnp.int32, (ATT_SUB, width), 0)
                    for g in range(n_grp):
                        vis[sub, g] = pk >= first
        elif mode == "diag":
            pk0 = kj * DA_KEYS + lax.broadcasted_iota(jnp.int32, (ATT_SUB, width), 0)
            pq0 = qi * tq + (lax.broadcasted_iota(jnp.int32, (ATT_SUB, width), 1) & (DA_GROUP - 1))
            for sub in subs:
                pk = pk0 + sub * ATT_SUB
                for g in range(n_grp):
                    vis[sub, g] = (pk >= first) & ((pk >> 6) <= ((pq0 + g * DA_GROUP) >> 6))
        items = [(sub, g, h) for sub in subs for g in range(n_grp) for h in range(DA_HEADS)]

        def scores(item):
            sub, g, h = item
            return jnp.dot(k_ref[sub * ATT_SUB:(sub + 1) * ATT_SUB, h * LANES:(h + 1) * LANES],
                           qs[g * DA_HEADS + h], preferred_element_type=F32)

        def softmax(item, s):
            sub, g, h = item
            if (sub, g) in vis:
                s = jnp.where(vis[sub, g], s, NEG_INF)
            m_prev = m_s[g * DA_HEADS + h]
            m_new = jnp.maximum(m_prev, jnp.max(s, axis=0, keepdims=True))
            m_use = jnp.where(m_new == NEG_INF, 0.0, m_new)
            m_s[g * DA_HEADS + h] = m_new
            return jnp.exp2(m_prev - m_use), jnp.exp2(s - m_use).astype(BF16)

        def accumulate(item, alpha, e):
            sub, g, h = item
            vt = vt_ref[h * DA_VROWS:(h + 1) * DA_VROWS, sub * ATT_SUB:(sub + 1) * ATT_SUB]
            acc_s[g * DA_HEADS + h] = alpha * acc_s[g * DA_HEADS + h] + jnp.dot(vt, e, preferred_element_type=F32)

        n = len(items)
        s_live, e_live = {}, {}
        for t in range(n + DA_LEAD + 1):
            if t < n:
                s_live[t] = scores(items[t])
            if DA_LEAD + 1 <= t:
                accumulate(items[t - DA_LEAD - 1], *e_live.pop(t - DA_LEAD - 1))
            if DA_LEAD <= t < n + DA_LEAD:
                e_live[t - DA_LEAD] = softmax(items[t - DA_LEAD], s_live.pop(t - DA_LEAD))

    off_diag = jnp.logical_not(on_diag)
    pl.when(on_diag)(lambda: run("diag"))
    pl.when((kj == 0) & off_diag)(lambda: run("pad"))
    pl.when((kj != 0) & off_diag)(lambda: run("open"))

    @pl.when(kj == last)
    def _():
        lam = (jnp.exp(jnp.sum(lam_ref[0:1, :] * lam_ref[1:2, :], axis=1, keepdims=True))
               - jnp.exp(jnp.sum(lam_ref[2:3, :] * lam_ref[3:4, :], axis=1, keepdims=True)) + lam_init)
        for g in range(n_grp):
            for h in range(DA_HEADS):
                acc = acc_s[g * DA_HEADS + h]
                l = acc[DA_DV:DA_DV + 1, :]
                o = acc[:DA_DV, :] / jnp.where(l == 0.0, 1.0, l)
                d = o[:, :DA_GROUP] - lam * o[:, DA_GROUP:]
                ms = jnp.mean(d * d, axis=0, keepdims=True)
                y = d * lax.rsqrt(ms + NORM_EPS) * g_ref[...] * (1.0 - lam_init)
                o_ref[g * DA_GROUP:(g + 1) * DA_GROUP, h * DA_DV:(h + 1) * DA_DV] = y.T


def _tri_steps(nq, tq, keys, descending):
    qi, kj = [], []
    for i in range(nq):
        last = (i * tq + tq - 1) // keys
        ks = range(last, -1, -1) if descending else range(last + 1)
        for k in ks:
            qi.append(i)
            kj.append(k)
    return jnp.asarray(qi, jnp.int32), jnp.asarray(kj, jnp.int32)


def da_prompt(proj, kb, vt, lam4, subln_g, lam_init):
    b, lp, _ = proj.shape
    tq = DA_TQ
    n_grp = tq // DA_GROUP
    qi, kj = _tri_steps(lp // tq, tq, DA_KEYS, descending=False)
    kern = functools.partial(_da_prompt_kernel, tq=tq, lam_init=lam_init)
    grid_spec = pltpu.PrefetchScalarGridSpec(
        num_scalar_prefetch=2,
        grid=(b, qi.shape[0]),
        in_specs=[
            pl.BlockSpec((None, tq, DA_W), lambda bb, s, qi, kj: (bb, qi[s], 0)),
            pl.BlockSpec((None, DA_KEYS, DA_W), lambda bb, s, qi, kj: (bb, kj[s], 0)),
            pl.BlockSpec((None, DA_HEADS * DA_VROWS, DA_KEYS), lambda bb, s, qi, kj: (bb, 0, kj[s])),
            pl.BlockSpec((4, DA_DH), lambda bb, s, qi, kj: (0, 0)),
            pl.BlockSpec((DA_DV, 1), lambda bb, s, qi, kj: (0, 0)),
        ],
        out_specs=pl.BlockSpec((None, tq, DA_W), lambda bb, s, qi, kj: (bb, qi[s], 0)),
        scratch_shapes=[
            pltpu.VMEM((n_grp * DA_HEADS, LANES, 2 * DA_GROUP), BF16),
            pltpu.VMEM((n_grp * DA_HEADS, 1, 2 * DA_GROUP), F32),
            pltpu.VMEM((n_grp * DA_HEADS, DA_VROWS, 2 * DA_GROUP), F32),
        ],
    )
    return pl.pallas_call(
        kern,
        out_shape=jax.ShapeDtypeStruct((b, lp, DA_W), F32),
        grid_spec=grid_spec,
        compiler_params=_cparams(("parallel", "arbitrary")),
        name="da_prompt",
    )(qi, kj, proj, kb, vt, lam4, subln_g.reshape(DA_DV, 1))


def _da_sample_kernel(q_ref, kn_ref, vn_ref, kc_ref, vc_ref, lam_ref, g_ref, o_ref,
                      qs, m_s, l_s, acc_s, *, rows, lam_init):
    j = pl.program_id(1)

    @pl.when(j == 0)
    def _():
        for h, qh in enumerate(_da_split_q(q_ref[...] * (DA_DH ** -0.5))):
            qs[h] = qh
        m_s[...] = jnp.full(m_s.shape, NEG_INF, F32)
        l_s[...] = jnp.zeros(l_s.shape, F32)
        acc_s[...] = jnp.zeros(acc_s.shape, F32)

    for h in range(DA_HEADS):
        kh = kc_ref[:, h * LANES:(h + 1) * LANES].astype(BF16)
        vh = vc_ref[:, h * DA_DV:(h + 1) * DA_DV].astype(BF16)
        _da_update(h, qs[h], kh, vh, None, m_s, l_s, acc_s)

    @pl.when(j == pl.num_programs(1) - 1)
    def _():
        for h in range(DA_HEADS):
            kh = kn_ref[:, h * LANES:(h + 1) * LANES].astype(BF16)
            vh = vn_ref[:, h * DA_DV:(h + 1) * DA_DV].astype(BF16)
            _da_update(h, qs[h], kh, vh, None, m_s, l_s, acc_s)
        _da_finish(lam_ref, g_ref, o_ref, l_s, acc_s, rows, lam_init)


def da_sample(proj, cache_k, cache_v, lam4, subln_g, lam_init):
    db, s, _ = proj.shape
    past = cache_k.shape[1]
    tk = min(1024, past)
    kern = functools.partial(_da_sample_kernel, rows=s, lam_init=lam_init)
    return pl.pallas_call(
        kern,
        out_shape=jax.ShapeDtypeStruct((db, s, DA_W), F32),
        grid=(db, past // tk),
        in_specs=[
            pl.BlockSpec((None, s, DA_W), lambda b, j: (b, 0, 0)),
            pl.BlockSpec((None, s, DA_W), lambda b, j: (b, 0, 1)),
            pl.BlockSpec((None, s, DA_W), lambda b, j: (b, 0, 2)),
            pl.BlockSpec((None, tk, DA_W), lambda b, j: (b, j, 0)),
            pl.BlockSpec((None, tk, DA_W), lambda b, j: (b, j, 0)),
            pl.BlockSpec((4, DA_DH), lambda b, j: (0, 0)),
            pl.BlockSpec((1, DA_DV), lambda b, j: (0, 0)),
        ],
        out_specs=pl.BlockSpec((None, s, DA_W), lambda b, j: (b, 0, 0)),
        scratch_shapes=[
            pltpu.VMEM((DA_HEADS, 2 * s, LANES), BF16),
            pltpu.VMEM((DA_HEADS, 2 * s, 1), F32),
            pltpu.VMEM((DA_HEADS, 2 * s, 1), F32),
            pltpu.VMEM((DA_HEADS, 2 * s, DA_DV), F32),
        ],
        compiler_params=_cparams(("parallel", "arbitrary")),
        name="da_sample",
    )(proj, proj, proj, cache_k, cache_v, lam4, subln_g.reshape(1, DA_DV))


def _sb_update(h, qh, kh, vh, u, vis, r_s, acc_s):
    z = lax.dot_general(qh, kh, NT_DIMS, preferred_element_type=F32)
    lk = -_softplus(z)
    if vis is not None:
        lk = jnp.where(vis, lk, 0.0)
    hi = lk.astype(BF16)
    lo = (lk - hi.astype(F32)).astype(BF16)
    tail = (jnp.dot(hi, u, preferred_element_type=F32) + jnp.dot(lo, u, preferred_element_type=F32)) + r_s[h]
    a = jnp.exp(z + tail)
    if vis is not None:
        a = jnp.where(vis, a, 0.0)
    acc_s[h] = acc_s[h] + jnp.dot(a.astype(BF16), vh, preferred_element_type=F32)
    r_s[h] = r_s[h] + jnp.sum(lk, axis=1, keepdims=True)


def _sb_prompt_kernel(qi_ref, kj_ref, q_ref, k_ref, vt_ref, w_ref, o_ref, qs, r_s, acc_s, *, tq):
    step = pl.program_id(1)
    qi = qi_ref[step]
    kj = kj_ref[step]
    last = (qi * tq + tq - 1) // SB_KEYS
    on_diag = (kj + 1) * SB_KEYS > qi * tq
    n_sub = SB_KEYS // SB_ITEM

    @pl.when(kj == last)
    def _():
        q = q_ref[...] * (SB_DH ** -0.5 * LOG2E)
        for h in range(SB_HEADS):
            qs[h] = q[:, h * SB_DH:(h + 1) * SB_DH].T.astype(BF16)
        r_s[...] = jnp.zeros(r_s.shape, F32)
        acc_s[...] = jnp.zeros(acc_s.shape, F32)

    def run(masked):
        items = [(sub, h) for sub in range(n_sub - 1, -1, -1) for h in range(SB_HEADS)]
        if masked:
            pk0 = kj * SB_KEYS + lax.broadcasted_iota(jnp.int32, (SB_ITEM, tq), 0)
            pq = qi * tq + lax.broadcasted_iota(jnp.int32, (SB_ITEM, tq), 1)
            vis = [pk0 + sub * SB_ITEM < pq for sub in range(n_sub)]
        w2 = w_ref[...]
        n_blk = SB_ITEM // SB_CUM

        def logits(item):
            sub, h = item
            return jnp.dot(k_ref[sub * SB_ITEM:(sub + 1) * SB_ITEM, h * SB_DH:(h + 1) * SB_DH], qs[h],
                           preferred_element_type=F32)

        def split(item, z):
            sub, _ = item
            sp = jnp.where(z > SB_LINEAR, z, jnp.log(1.0 + jnp.exp2(z)) * LOG2E)
            if masked:
                sp = jnp.where(vis[sub], sp, 0.0)
            hi = sp.astype(BF16)
            return hi, (sp - hi.astype(F32)).astype(BF16)

        def suffix(hi, lo):
            return [jnp.dot(w2, jnp.concatenate([hi[b * SB_CUM:(b + 1) * SB_CUM], lo[b * SB_CUM:(b + 1) * SB_CUM]],
                                                axis=0), preferred_element_type=F32) for b in range(n_blk)]

        def weights(item, z, tins):
            sub, h = item
            r = r_s[h]
            tails = [None] * n_blk
            for b in range(n_blk - 1, -1, -1):
                tails[b] = tins[b] + r
                r = r + tins[b][0:1, :]
            r_s[h] = r
            a = jnp.exp2(z - jnp.concatenate(tails, axis=0))
            if masked:
                a = jnp.where(vis[sub], a, 0.0)
            return a.astype(BF16)

        def accumulate(item, a):
            sub, h = item
            vt = vt_ref[h * SB_DH:(h + 1) * SB_DH, sub * SB_ITEM:(sub + 1) * SB_ITEM]
            acc_s[h] = acc_s[h] + jnp.dot(vt, a, preferred_element_type=F32)

        n = len(items)
        z_live, t_live = {}, {}
        for t in range(n + 2):
            if t < n:
                z_live[t] = logits(items[t])
            if 1 <= t < n + 1:
                t_live[t - 1] = suffix(*split(items[t - 1], z_live[t - 1]))
            if t >= 2:
                accumulate(items[t - 2], weights(items[t - 2], z_live.pop(t - 2), t_live.pop(t - 2)))

    pl.when(on_diag)(lambda: run(True))
    pl.when(jnp.logical_not(on_diag))(lambda: run(False))

    @pl.when(kj == 0)
    def _():
        for h in range(SB_HEADS):
            o_ref[:, h * SB_DH:(h + 1) * SB_DH] = acc_s[h].T


def _tri_ge(n, transpose=False):
    idx = jnp.arange(n)
    m = idx[:, None] >= idx[None, :]
    return (m.T if transpose else m).astype(BF16)


def sb_prompt(proj, kb, vt):
    b, lp, _ = proj.shape
    tq = SB_TQ
    qi, kj = _tri_steps(lp // tq, tq, SB_KEYS, descending=True)
    kern = functools.partial(_sb_prompt_kernel, tq=tq)
    grid_spec = pltpu.PrefetchScalarGridSpec(
        num_scalar_prefetch=2,
        grid=(b, qi.shape[0]),
        in_specs=[
            pl.BlockSpec((None, tq, SB_W), lambda bb, s, qi, kj: (bb, qi[s], 0)),
            pl.BlockSpec((None, SB_KEYS, SB_W), lambda bb, s, qi, kj: (bb, kj[s], 0)),
            pl.BlockSpec((None, SB_W, SB_KEYS), lambda bb, s, qi, kj: (bb, 0, kj[s])),
            pl.BlockSpec((SB_CUM, 2 * SB_CUM), lambda bb, s, qi, kj: (0, 0)),
        ],
        out_specs=pl.BlockSpec((None, tq, SB_W), lambda bb, s, qi, kj: (bb, qi[s], 0)),
        scratch_shapes=[
            pltpu.VMEM((SB_HEADS, SB_DH, tq), BF16),
            pltpu.VMEM((SB_HEADS, 1, tq), F32),
            pltpu.VMEM((SB_HEADS, SB_DH, tq), F32),
        ],
    )
    return pl.pallas_call(
        kern,
        out_shape=jax.ShapeDtypeStruct((b, lp, SB_W), F32),
        grid_spec=grid_spec,
        compiler_params=_cparams(("parallel", "arbitrary")),
        name="sb_prompt",
    )(qi, kj, proj, kb, vt, jnp.tile(_tri_ge(SB_CUM, transpose=True), (1, 2)))


def _sb_sample_kernel(q_ref, kn_ref, vn_ref, kc_ref, vc_ref, un_ref, uc_ref, o_ref, qs, r_s, acc_s, *, rows):
    j = pl.program_id(1)

    @pl.when(j == 0)
    def _():
        q = q_ref[...] * (SB_DH ** -0.5)
        for h in range(SB_HEADS):
            qs[h] = q[:, h * SB_DH:(h + 1) * SB_DH].astype(BF16)
        r_s[...] = jnp.zeros(r_s.shape, F32)
        acc_s[...] = jnp.zeros(acc_s.shape, F32)
        iq = lax.broadcasted_iota(jnp.int32, (rows, rows), 0)
        ik = lax.broadcasted_iota(jnp.int32, (rows, rows), 1)
        vis = ik < iq
        un = un_ref[...]
        for h in range(SB_HEADS):
            kh = kn_ref[:, h * SB_DH:(h + 1) * SB_DH].astype(BF16)
            vh = vn_ref[:, h * SB_DH:(h + 1) * SB_DH].astype(BF16)
            _sb_update(h, qs[h], kh, vh, un, vis, r_s, acc_s)

    uc = uc_ref[...]
    for h in range(SB_HEADS):
        kh = kc_ref[:, h * SB_DH:(h + 1) * SB_DH].astype(BF16)
        vh = vc_ref[:, h * SB_DH:(h + 1) * SB_DH].astype(BF16)
        _sb_update(h, qs[h], kh, vh, uc, None, r_s, acc_s)

    @pl.when(j == pl.num_programs(1) - 1)
    def _():
        for h in range(SB_HEADS):
            o_ref[:, h * SB_DH:(h + 1) * SB_DH] = acc_s[h]


def sb_sample(proj, cache_k, cache_v):
    db, s, _ = proj.shape
    past = cache_k.shape[1]
    tk = min(512, past)
    n = past // tk
    kern = functools.partial(_sb_sample_kernel, rows=s)
    return pl.pallas_call(
        kern,
        out_shape=jax.ShapeDtypeStruct((db, s, SB_W), F32),
        grid=(db, n),
        in_specs=[
            pl.BlockSpec((None, s, SB_W), lambda b, j: (b, 0, 0)),
            pl.BlockSpec((None, s, SB_W), lambda b, j: (b, 0, 1)),
            pl.BlockSpec((None, s, SB_W), lambda b, j: (b, 0, 2)),
            pl.BlockSpec((None, tk, SB_W), lambda b, j: (b, n - 1 - j, 0)),
            pl.BlockSpec((None, tk, SB_W), lambda b, j: (b, n - 1 - j, 0)),
            pl.BlockSpec((s, s), lambda b, j: (0, 0)),
            pl.BlockSpec((tk, tk), lambda b, j: (0, 0)),
        ],
        out_specs=pl.BlockSpec((None, s, SB_W), lambda b, j: (b, 0, 0)),
        scratch_shapes=[
            pltpu.VMEM((SB_HEADS, s, SB_DH), BF16),
            pltpu.VMEM((SB_HEADS, s, 1), F32),
            pltpu.VMEM((SB_HEADS, s, SB_DH), F32),
        ],
        compiler_params=_cparams(("parallel", "arbitrary")),
        name="sb_sample",
    )(proj, proj, proj, cache_k, cache_v, _tri_ge(s), _tri_ge(tk))


def _ret_block(q, k, v, cos, sin, state, log_gamma):
    c = q.shape[0]
    qr = _rotary(q, cos, sin)
    kr = _rotary(k, cos, sin) * (RET_DK ** -0.5)
    ii = lax.broadcasted_iota(jnp.int32, (c, c), 0)
    jj = lax.broadcasted_iota(jnp.int32, (c, c), 1)
    rel = (ii - jj).astype(F32)
    decay = jnp.where(rel >= 0.0, jnp.exp(jnp.maximum(rel, 0.0) * log_gamma), 0.0)
    idx = lax.broadcasted_iota(jnp.int32, (c, 1), 0).astype(F32)
    q_dec = (qr * jnp.exp((idx + 1.0) * log_gamma)).astype(BF16)
    k_dec = (kr * jnp.exp((c - 1.0 - idx) * log_gamma)).astype(BF16)
    vb = v.astype(BF16)
    scores = lax.dot_general(qr.astype(BF16), kr.astype(BF16), NT_DIMS, preferred_element_type=F32) * decay
    o = (jnp.dot(scores.astype(BF16), vb, preferred_element_type=F32)
         + jnp.dot(q_dec, state.astype(BF16), preferred_element_type=F32))
    new_state = (jnp.exp(c * log_gamma) * state
                 + lax.dot_general(k_dec, vb, TN_DIMS, preferred_element_type=F32))
    return o, new_state


def _ret_norm(o, g):
    ms = jnp.mean(o * o, axis=1, keepdims=True)
    return o * lax.rsqrt(ms + NORM_EPS) * g


def _log_gamma(h):
    hv = jnp.zeros((1, 1), F32) + (float(h) if isinstance(h, int) else h.astype(F32))
    return jnp.log(1.0 - jnp.exp2(-5.0 - hv))


def _ret_prompt_kernel(q_ref, k_ref, v_ref, cos_ref, sin_ref, g_ref, o_ref, sf_ref, st):
    j = pl.program_id(1)
    c = q_ref.shape[0]

    @pl.when(j == 0)
    def _():
        st[...] = jnp.zeros(st.shape, F32)

    cos, sin = cos_ref[...], sin_ref[...]
    ii = lax.broadcasted_iota(jnp.int32, (c, c), 0)
    jj = lax.broadcasted_iota(jnp.int32, (c, c), 1)
    rel = jnp.maximum(ii - jj, 0).astype(F32)
    idx = lax.broadcasted_iota(jnp.int32, (c, 1), 0).astype(F32)
    heads = range(RET_HEADS)
    lg = [_log_gamma(h) for h in heads]
    qr = [_rotary(q_ref[:, h * RET_DK:(h + 1) * RET_DK], cos, sin) for h in heads]
    kr = [_rotary(k_ref[:, h * RET_DK:(h + 1) * RET_DK], cos, sin) * (RET_DK ** -0.5) for h in heads]
    vb = [v_ref[:, h * RET_DV:(h + 1) * RET_DV].astype(BF16) for h in heads]
    scores = [lax.dot_general(qr[h].astype(BF16), kr[h].astype(BF16), NT_DIMS, preferred_element_type=F32)
              for h in heads]
    cross = [jnp.dot((qr[h] * jnp.exp((idx + 1.0) * lg[h])).astype(BF16), st[h].astype(BF16),
                     preferred_element_type=F32) for h in heads]
    grown = [lax.dot_general((kr[h] * jnp.exp((c - 1.0 - idx) * lg[h])).astype(BF16), vb[h], TN_DIMS,
                             preferred_element_type=F32) for h in heads]
    for h in heads:
        decay = jnp.where(ii >= jj, jnp.exp(rel * lg[h]), 0.0)
        o = jnp.dot((scores[h] * decay).astype(BF16), vb[h], preferred_element_type=F32) + cross[h]
        o_ref[:, h * RET_DV:(h + 1) * RET_DV] = _ret_norm(o, g_ref[h])
        st[h] = jnp.exp(c * lg[h]) * st[h] + grown[h]

    @pl.when(j == pl.num_programs(1) - 1)
    def _():
        sf_ref[...] = st[...]


def ret_prompt(proj, cos, sin, gn_g):
    b, lp, _ = proj.shape
    c = RET_BLOCK
    qk_w = RET_HEADS * RET_DK
    v_w = RET_HEADS * RET_DV
    return pl.pallas_call(
        _ret_prompt_kernel,
        out_shape=(jax.ShapeDtypeStruct((b, lp, v_w), F32),
                   jax.ShapeDtypeStruct((b, RET_HEADS, RET_DK, RET_DV), F32)),
        grid=(b, lp // c),
        in_specs=[
            pl.BlockSpec((None, c, qk_w), lambda bb, j: (bb, j, 0)),
            pl.BlockSpec((None, c, qk_w), lambda bb, j: (bb, j, 1)),
            pl.BlockSpec((None, c, v_w), lambda bb, j: (bb, j, 1)),
            pl.BlockSpec((c, RET_DK // 2), lambda bb, j: (j, 0)),
            pl.BlockSpec((c, RET_DK // 2), lambda bb, j: (j, 0)),
            pl.BlockSpec((RET_HEADS, 1, RET_DV), lambda bb, j: (0, 0, 0)),
        ],
        out_specs=(pl.BlockSpec((None, c, v_w), lambda bb, j: (bb, j, 0)),
                   pl.BlockSpec((None, RET_HEADS, RET_DK, RET_DV), lambda bb, j: (bb, 0, 0, 0))),
        scratch_shapes=[pltpu.VMEM((RET_HEADS, RET_DK, RET_DV), F32)],
        compiler_params=_cparams(("parallel", "arbitrary")),
        name="ret_prompt",
    )(proj, proj, proj, cos, sin, gn_g.reshape(RET_HEADS, 1, RET_DV))


def _ret_sample_kernel(q_ref, k_ref, v_ref, cos_ref, sin_ref, g_ref, s_ref, o_ref, sf_ref):
    o, new_state = _ret_block(q_ref[...], k_ref[...], v_ref[...], cos_ref[...], sin_ref[...], s_ref[...],
                              _log_gamma(pl.program_id(1)))
    o_ref[...] = _ret_norm(o, g_ref[...])
    sf_ref[...] = new_state


def ret_sample(proj, cos, sin, gn_g, state):
    db, s, _ = proj.shape
    kv_off = (RET_HEADS * RET_DK) // RET_DK
    v_off = (2 * RET_HEADS * RET_DK) // RET_DV
    return pl.pallas_call(
        _ret_sample_kernel,
        out_shape=(jax.ShapeDtypeStruct((db, s, RET_HEADS * RET_DV), F32),
                   jax.ShapeDtypeStruct(state.shape, F32)),
        grid=(db, RET_HEADS),
        in_specs=[
            pl.BlockSpec((None, s, RET_DK), lambda b, h: (b, 0, h)),
            pl.BlockSpec((None, s, RET_DK), lambda b, h: (b, 0, kv_off + h)),
            pl.BlockSpec((None, s, RET_DV), lambda b, h: (b, 0, v_off + h)),
            pl.BlockSpec((s, RET_DK // 2), lambda b, h: (0, 0)),
            pl.BlockSpec((s, RET_DK // 2), lambda b, h: (0, 0)),
            pl.BlockSpec((None, 1, RET_DV), lambda b, h: (h, 0, 0)),
            pl.BlockSpec((None, None, RET_DK, RET_DV), lambda b, h: (b, h, 0, 0)),
        ],
        out_specs=(pl.BlockSpec((None, s, RET_DV), lambda b, h: (b, 0, h)),
                   pl.BlockSpec((None, None, RET_DK, RET_DV), lambda b, h: (b, h, 0, 0))),
        compiler_params=_cparams(("parallel", "parallel")),
        name="ret_sample",
    )(proj, proj, proj, cos, sin, gn_g.reshape(RET_HEADS, 1, RET_DV), state)


def kernel(x_prompt, x_sample, cache_k_l0, cache_v_l0, state_ret_l1, cache_k_l2, cache_v_l2, cache_k_l3, cache_v_l3, meta_tokens, norm_g_0, w_in_0, lam_q1_0, lam_k1_0, lam_q2_0, lam_k2_0, subln_g_0, w_out_0, norm_g_1, w_in_1, gn_g_1, w_out_1, norm_g_2, w_in_2, w_out_2, norm_g_3, w_in_3, lam_q1_3, lam_k1_3, lam_q2_3, lam_k2_3, subln_g_3, w_out_3, norm_g_final):
    b, seq, d = x_prompt.shape
    db, s_len, _ = x_sample.shape
    past = cache_k_l0.shape[1]
    assert seq % FRONT_PAD == 0 and N_META <= FRONT_PAD
    lp = FRONT_PAD + seq
    n_real = N_META + seq
    first = FRONT_PAD - N_META

    meta = jnp.broadcast_to(meta_tokens.astype(F32)[None], (b, N_META, d))
    xp = jnp.concatenate([jnp.zeros((b, first, d), F32), meta, x_prompt], axis=1).reshape(b * lp, d)
    xs = x_sample.reshape(db * s_len, d)

    layers = [
        (norm_g_0, w_in_0, w_out_0, (lam_q1_0, lam_k1_0, lam_q2_0, lam_k2_0, subln_g_0), (cache_k_l0, cache_v_l0)),
        (norm_g_1, w_in_1, w_out_1, (gn_g_1,), (state_ret_l1,)),
        (norm_g_2, w_in_2, w_out_2, (), (cache_k_l2, cache_v_l2)),
        (norm_g_3, w_in_3, w_out_3, (lam_q1_3, lam_k1_3, lam_q2_3, lam_k2_3, subln_g_3), (cache_k_l3, cache_v_l3)),
    ]
    states = []
    y_prompt = y_sample = None
    n_layers = len(layers)
    for i, (norm_g, w_in, w_out, extra, cache) in enumerate(layers):
        kind = i % N_MIXERS
        w_in_b = w_in.astype(BF16)
        w_out_b = w_out.astype(BF16)
        ps = norm_proj(xs, norm_g, w_in_b)
        e = w_in.shape[1]
        ps3 = ps.reshape(db, s_len, e)
        if kind == 1:
            pp = norm_proj(xp, norm_g, w_in_b)
            gate_p, gate_block_p, gate_block_s = pp, 2, 2
            cos_p, sin_p = rope_tables(lp, -FRONT_PAD)
            cos_s, sin_s = rope_tables(s_len, past)
            op, sp = ret_prompt(pp.reshape(b, lp, e), cos_p, sin_p, extra[0])
            os_, ss = ret_sample(ps3, cos_s, sin_s, extra[0], cache[0])
            states.append((sp, ss))
        else:
            wdt, heads = (DA_W, DA_HEADS) if kind == 0 else (SB_W, SB_HEADS)
            qp, kp, vp, gate_p, kb, vt = norm_proj_kv(xp, norm_g, w_in_b, b, wdt, heads,
                                                      BF16_ROWS if kind == 0 else 0)
            gate_block_p, gate_block_s = 0, 3
            q3 = qp.reshape(b, lp, wdt)
            kb = kb.reshape(b, lp, wdt)
            ck = cache[0].reshape(db, past, wdt)
            cv = cache[1].reshape(db, past, wdt)
            if kind == 0:
                lam_init = 0.8 - 0.6 * math.exp(-0.3 * i)
                lam4 = jnp.stack([v.astype(F32) for v in extra[:4]])
                op = da_prompt(q3, kb, vt, lam4, extra[4], lam_init)
                os_ = da_sample(ps3, ck, cv, lam4, extra[4], lam_init)
                k_shape, v_shape = (2 * DA_HEADS, DA_DH), (DA_HEADS, DA_DV)
            else:
                op = sb_prompt(q3, kb, vt)
                os_ = sb_sample(ps3, ck, cv)
                k_shape, v_shape = (SB_HEADS, SB_DH), (SB_HEADS, SB_DH)
            states.append((
                kp.reshape(b, lp, wdt)[:, first:].reshape(b, n_real, *k_shape),
                vp.reshape(b, lp, wdt)[:, first:].reshape(b, n_real, *v_shape),
                ps3[:, :, wdt:2 * wdt].reshape(db, s_len, *k_shape),
                ps3[:, :, 2 * wdt:3 * wdt].reshape(db, s_len, *v_shape)))
        wdt = op.shape[-1]
        op2 = op.reshape(b * lp, wdt)
        os2 = os_.reshape(db * s_len, wdt)
        if i + 1 < n_layers:
            xp = post_proj(op2, gate_p, gate_block_p, xp, w_out_b)
            xs = post_proj(os2, ps, gate_block_s, xs, w_out_b)
        else:
            y_sample = post_proj(os2, ps, gate_block_s, xs, w_out_b, final_g=norm_g_final).reshape(db, s_len, d)
            y_prompt = post_proj(op2, gate_p, gate_block_p, xp, w_out_b, batch=b, final_g=norm_g_final,
                                 skip_rows=FRONT_PAD).reshape(b, seq, d)

    k0_p, v0_p, k0_s, v0_s = states[0]
    ret1_p, ret1_s = states[1]
    k2_p, v2_p, k2_s, v2_s = states[2]
    k3_p, v3_p, k3_s, v3_s = states[3]
    return (y_prompt, y_sample, k0_p, v0_p, k0_s, v0_s, ret1_p, ret1_s,
            k2_p, v2_p, k2_s, v2_s, k3_p, v3_p, k3_s, v3_s)
```
